```python
import jax, jax.numpy as jnp
from jax import lax
import numpy as np

D_MODEL = 1024
BATCH = 16
SEQ = 2048
DEPTH = 1

GRID_W = 64
HEAD_DIM = 64
N_Q_HEADS = 8
N_KV_HEADS = 2
Q_PER_KV = N_Q_HEADS // N_KV_HEADS
ATTN_WIDTH = N_Q_HEADS * HEAD_DIM
KV_WIDTH = N_KV_HEADS * HEAD_DIM
LRU_WIDTH = D_MODEL - ATTN_WIDTH
LRU_BLOCKS = 8
LRU_BLOCK = LRU_WIDTH // LRU_BLOCKS
CONV_WIDTH = 4
LRU_C = 8.0
D_MIX = ATTN_WIDTH + LRU_WIDTH
D_IN = ATTN_WIDTH + 2 * KV_WIDTH + 2 * LRU_WIDTH
D_FF = -(-8 * D_MODEL // (3 * 256)) * 256
Q_BLOCK = 128
ROPE_THETA = 10000.0
EPS = 1e-6

kernel_name = "hymba_attn_rglru_hybrid_encoder"


def rms_norm(x, g):
    x32 = x.astype(jnp.float32)
    y = x32 * lax.rsqrt(jnp.mean(x32 * x32, axis=-1, keepdims=True) + EPS)
    return (y * g.astype(jnp.float32)).astype(x.dtype)


def axial_rope_tables(S):
    rows = S // GRID_W
    row = jnp.repeat(jnp.arange(rows, dtype=jnp.float32), GRID_W)
    col = jnp.tile(jnp.arange(GRID_W, dtype=jnp.float32), rows)
    axis_dim = HEAD_DIM // 2
    inv = ROPE_THETA ** (-jnp.arange(0, axis_dim, 2, dtype=jnp.float32) / axis_dim)
    ang_r = row[:, None] * inv[None, :]
    ang_c = col[:, None] * inv[None, :]
    ang = jnp.concatenate([ang_r, ang_r, ang_c, ang_c], axis=-1)
    return jnp.cos(ang), jnp.sin(ang)


def _rotate_half(z):
    z1, z2 = jnp.split(z, 2, axis=-1)
    return jnp.concatenate([-z2, z1], axis=-1)


def apply_axial_rope(x, cos, sin):
    x32 = x.astype(jnp.float32)
    xr, xc = jnp.split(x32, 2, axis=-1)
    rot = jnp.concatenate([_rotate_half(xr), _rotate_half(xc)], axis=-1)
    out = x32 * cos[None, :, None, :] + rot * sin[None, :, None, :]
    return out.astype(x.dtype)


def block_attention(q, k, v):
    B, S = q.shape[0], q.shape[1]
    nblk = S // Q_BLOCK
    qb = q.reshape(B, nblk, Q_BLOCK, N_KV_HEADS, Q_PER_KV, HEAD_DIM).transpose(1, 0, 3, 4, 2, 5)
    kt = k.transpose(0, 2, 1, 3)
    vt = v.transpose(0, 2, 1, 3)
    scale = HEAD_DIM ** -0.5

    def one_block(qblk):
        s = jnp.einsum('bkgqd,bksd->bkgqs', qblk, kt,
                       preferred_element_type=jnp.float32) * scale
        p = jax.nn.softmax(s, axis=-1).astype(vt.dtype)
        return jnp.einsum('bkgqs,bksd->bkgqd', p, vt)

    ob = lax.map(one_block, qb)
    return ob.transpose(1, 0, 4, 2, 3, 5).reshape(B, S, ATTN_WIDTH)


def centred_depthwise_conv(x, w, b):
    S = x.shape[1]
    left = (CONV_WIDTH - 1) // 2
    xp = jnp.pad(x, ((0, 0), (left, CONV_WIDTH - 1 - left), (0, 0)))
    out = b[None, None, :]
    for j in range(CONV_WIDTH):
        out = out + xp[:, j:j + S, :] * w[j][None, None, :]
    return out


def _linear_combine(c1, c2):
    a1, b1 = c1
    a2, b2 = c2
    return a1 * a2, a2 * b1 + b2


def rg_lru_direction(x, w_r, b_r, w_i, b_i, lam, reverse):
    B, S, W = x.shape
    xb = x.reshape(B, S, LRU_BLOCKS, LRU_BLOCK)
    r = jax.nn.sigmoid(jnp.einsum('bsnc,ncd->bsnd', xb, w_r.astype(jnp.float32)).reshape(B, S, W)
                       + b_r.astype(jnp.float32))
    i = jax.nn.sigmoid(jnp.einsum('bsnc,ncd->bsnd', xb, w_i.astype(jnp.float32)).reshape(B, S, W)
                       + b_i.astype(jnp.float32))
    log_a = -LRU_C * r * jax.nn.softplus(-lam.astype(jnp.float32))
    a = jnp.exp(log_a)
    inp = jnp.sqrt(-jnp.expm1(2.0 * log_a)) * (i * x)
    _, h = lax.associative_scan(_linear_combine, (a, inp), reverse=reverse, axis=1)
    return h


def setup_inputs(seed: int = 0) -> dict:
    key = jax.random.key(seed)
    ks = jax.random.split(key, 24)
    f32 = jnp.float32

    def nrm(k, shape, fan_in, extra=1.0):
        return jax.random.normal(k, shape, f32) * (fan_in ** -0.5) * extra

    def gain(k, shape):
        return 1.0 + 0.02 * jax.random.normal(k, shape, f32)

    def bias(k, shape):
        return 0.02 * jax.random.normal(k, shape, f32)

    res_scale = (2.0 * DEPTH) ** -0.5
    x = jax.random.normal(ks[0], (BATCH, SEQ, D_MODEL), f32)
    norm_mix = gain(ks[1], (DEPTH, D_MODEL))
    w_in = nrm(ks[2], (DEPTH, D_MODEL, D_IN), D_MODEL)
    q_norm = gain(ks[3], (DEPTH, HEAD_DIM))
    k_norm = gain(ks[4], (DEPTH, HEAD_DIM))
    conv_w = nrm(ks[5], (DEPTH, CONV_WIDTH, LRU_WIDTH), CONV_WIDTH)
    conv_b = bias(ks[6], (DEPTH, LRU_WIDTH))
    w_rgate = nrm(ks[7], (DEPTH, 2, LRU_BLOCKS, LRU_BLOCK, LRU_BLOCK), LRU_BLOCK)
    b_rgate = bias(ks[8], (DEPTH, 2, LRU_WIDTH))
    w_igate = nrm(ks[9], (DEPTH, 2, LRU_BLOCKS, LRU_BLOCK, LRU_BLOCK), LRU_BLOCK)
    b_igate = bias(ks[10], (DEPTH, 2, LRU_WIDTH))
    a_pow_c = jax.random.uniform(ks[11], (DEPTH, 2, LRU_WIDTH), f32, 0.9, 0.999)
    a_base = a_pow_c ** (1.0 / LRU_C)
    lru_lambda = jnp.log(a_base) - jnp.log1p(-a_base)
    out_norm_attn = gain(ks[12], (DEPTH, ATTN_WIDTH))
    out_norm_lru = gain(ks[13], (DEPTH, LRU_WIDTH))
    w_out = nrm(ks[14], (DEPTH, D_MIX, D_MODEL), D_MIX, res_scale)
    norm_ffn = gain(ks[15], (DEPTH, D_MODEL))
    w_gate = nrm(ks[16], (DEPTH, D_MODEL, D_FF), D_MODEL)
    w_up = nrm(ks[17], (DEPTH, D_MODEL, D_FF), D_MODEL)
    w_down = nrm(ks[18], (DEPTH, D_FF, D_MODEL), D_FF, res_scale)
    return {"x": x, "norm_mix": norm_mix, "w_in": w_in, "q_norm": q_norm, "k_norm": k_norm,
            "conv_w": conv_w, "conv_b": conv_b, "w_rgate": w_rgate, "b_rgate": b_rgate,
            "w_igate": w_igate, "b_igate": b_igate, "lru_lambda": lru_lambda,
            "out_norm_attn": out_norm_attn, "out_norm_lru": out_norm_lru, "w_out": w_out,
            "norm_ffn": norm_ffn, "w_gate": w_gate, "w_up": w_up, "w_down": w_down}


def reference(x, norm_mix, w_in, q_norm, k_norm, conv_w, conv_b, w_rgate, b_rgate,
              w_igate, b_igate, lru_lambda, out_norm_attn, out_norm_lru, w_out,
              norm_ffn, w_gate, w_up, w_down):
    B, S, _ = x.shape
    cos, sin = axial_rope_tables(S)
    splits = [ATTN_WIDTH, ATTN_WIDTH + KV_WIDTH, ATTN_WIDTH + 2 * KV_WIDTH,
              ATTN_WIDTH + 2 * KV_WIDTH + LRU_WIDTH]
    h = x
    for l in range(DEPTH):
        u = rms_norm(h, norm_mix[l])
        proj = jnp.einsum('bsd,de->bse', u, w_in[l])
        q, k, v, xl, gl = jnp.split(proj, splits, axis=-1)
        q = q.reshape(B, S, N_Q_HEADS, HEAD_DIM)
        k = k.reshape(B, S, N_KV_HEADS, HEAD_DIM)
        v = v.reshape(B, S, N_KV_HEADS, HEAD_DIM)
        q = apply_axial_rope(rms_norm(q, q_norm[l]), cos, sin)
        k = apply_axial_rope(rms_norm(k, k_norm[l]), cos, sin)
        attn_out = block_attention(q, k, v)

        xc = centred_depthwise_conv(xl, conv_w[l], conv_b[l]).astype(jnp.float32)
        y_fwd = rg_lru_direction(xc, w_rgate[l, 0], b_rgate[l, 0], w_igate[l, 0],
                                 b_igate[l, 0], lru_lambda[l, 0], reverse=False)
        y_bwd = rg_lru_direction(xc, w_rgate[l, 1], b_rgate[l, 1], w_igate[l, 1],
                                 b_igate[l, 1], lru_lambda[l, 1], reverse=True)
        lru_out = ((y_fwd + y_bwd) * jax.nn.gelu(gl.astype(jnp.float32))).astype(h.dtype)

        mixed = jnp.concatenate([rms_norm(attn_out, out_norm_attn[l]),
                                 rms_norm(lru_out, out_norm_lru[l])], axis=-1)
        h = h + jnp.einsum('bse,ed->bsd', mixed, w_out[l])

        u = rms_norm(h, norm_ffn[l])
        ff = jax.nn.silu(jnp.einsum('bsd,df->bsf', u, w_gate[l])) * jnp.einsum('bsd,df->bsf', u, w_up[l])
        h = h + jnp.einsum('bsf,fd->bsd', ff, w_down[l])
    return h
```

```python
import functools

import jax
import jax.numpy as jnp
from jax import lax
from jax.experimental import pallas as pl
from jax.experimental.pallas import tpu as pltpu

F32 = jnp.float32
BF16 = jnp.bfloat16

D_MODEL = 1024
GRID_W = 64
HEAD_DIM = 64
N_Q_HEADS = 8
N_KV_HEADS = 2
ATTN_WIDTH = N_Q_HEADS * HEAD_DIM
KV_WIDTH = N_KV_HEADS * HEAD_DIM
LRU_WIDTH = D_MODEL - ATTN_WIDTH
LRU_BLOCKS = 8
LRU_BLOCK = LRU_WIDTH // LRU_BLOCKS
CONV_WIDTH = 4
LRU_C = 8.0
D_IN = ATTN_WIDTH + 2 * KV_WIDTH + 2 * LRU_WIDTH
ROPE_THETA = 10000.0
EPS = 1e-6

LANES = 128
SUBLANES = 8
MXU_DIM = 256
VMEM_LIMIT = 56 * 1024 * 1024

TM_PROJ = 512
TQ = 256
FF_CHUNK = MXU_DIM
GATE_ROWS = 128
OUT_ROWS = 64
N_SEG = SUBLANES


def _rms(x, gain):
    return x * lax.rsqrt(jnp.mean(x * x, axis=-1, keepdims=True) + EPS) * gain


def _inproj_kernel(x_ref, g_ref, w_ref, cos_ref, sa_ref, sb_ref, qg_ref, kg_ref, bd_ref,
                   q_ref, kt_ref, v_ref, xl_ref, gl_ref):
    u = _rms(x_ref[...], g_ref[...]).astype(BF16)
    cos = cos_ref[...]
    sin_a = sa_ref[...]
    sin_b = sb_ref[...]
    bd = bd_ref[...]
    lane = lax.broadcasted_iota(jnp.int32, (1, LANES), 1)
    low = lane < HEAD_DIM

    def norm_rope(z, gain):
        sq = z * z
        hi = sq.astype(BF16)
        lo = (sq - hi.astype(F32)).astype(BF16)
        ss = jnp.dot(jnp.concatenate([hi, lo], axis=-1), bd, preferred_element_type=F32)
        y = z * lax.rsqrt(ss * (1.0 / HEAD_DIM) + EPS) * gain
        return (y * cos + pltpu.roll(y, LANES - 16, 1) * sin_a
                + pltpu.roll(y, 16, 1) * sin_b)

    qg = qg_ref[...]
    for c in range(ATTN_WIDTH // LANES):
        z = jnp.dot(u, w_ref[:, c * LANES:(c + 1) * LANES], preferred_element_type=F32)
        q_ref[:, c * LANES:(c + 1) * LANES] = norm_rope(z, qg).astype(BF16)

    o = ATTN_WIDTH
    k2 = norm_rope(jnp.dot(u, w_ref[:, o:o + KV_WIDTH], preferred_element_type=F32), kg_ref[...])
    o += KV_WIDTH
    v2 = jnp.dot(u, w_ref[:, o:o + KV_WIDTH], preferred_element_type=F32)
    o += KV_WIDTH

    def expand(z):
        a0 = jnp.where(low, z, 0.0)
        b1 = jnp.where(low, 0.0, z)
        return jnp.concatenate([a0, pltpu.roll(a0, HEAD_DIM, 1), pltpu.roll(b1, HEAD_DIM, 1), b1], axis=-1)

    kt_ref[0] = expand(k2).T.astype(BF16)
    v_ref[...] = expand(v2).astype(BF16)
    xl_ref[...] = jnp.dot(u, w_ref[:, o:o + LRU_WIDTH], preferred_element_type=F32)
    o += LRU_WIDTH
    gl_ref[...] = jnp.dot(u, w_ref[:, o:o + LRU_WIDTH], preferred_element_type=F32)


def _inproj(x2, norm_g, w_in, cos2, sin_a, sin_b, qg, kg, bd, batch, seq):
    t = x2.shape[0]
    tm = TM_PROJ
    per_seq = seq // tm
    row = lambda i: (i, 0)
    const = lambda i: (0, 0)
    pos = lambda i: (i % per_seq, 0)
    return pl.pallas_call(
        _inproj_kernel,
        grid=(t // tm,),
        in_specs=[
            pl.BlockSpec((tm, D_MODEL), row),
            pl.BlockSpec((1, D_MODEL), const),
            pl.BlockSpec((D_MODEL, D_IN), const),
            pl.BlockSpec((tm, LANES), pos),
            pl.BlockSpec((tm, LANES), pos),
            pl.BlockSpec((tm, LANES), pos),
            pl.BlockSpec((1, LANES), const),
            pl.BlockSpec((1, LANES), const),
            pl.BlockSpec((2 * LANES, LANES), const),
        ],
        out_specs=[
            pl.BlockSpec((tm, ATTN_WIDTH), row),
            pl.BlockSpec((1, 4 * LANES, tm), lambda i: (i // per_seq, 0, i % per_seq)),
            pl.BlockSpec((tm, 4 * LANES), row),
            pl.BlockSpec((tm, LRU_WIDTH), row),
            pl.BlockSpec((tm, LRU_WIDTH), row),
        ],
        out_shape=[
            jax.ShapeDtypeStruct((t, ATTN_WIDTH), BF16),
            jax.ShapeDtypeStruct((batch, 4 * LANES, seq), BF16),
            jax.ShapeDtypeStruct((t, 4 * LANES), BF16),
            jax.ShapeDtypeStruct((t, LRU_WIDTH), F32),
            jax.ShapeDtypeStruct((t, LRU_WIDTH), F32),
        ],
        compiler_params=pltpu.CompilerParams(
            dimension_semantics=("arbitrary",), vmem_limit_bytes=VMEM_LIMIT),
        name="inproj",
    )(x2, norm_g, w_in, cos2, sin_a, sin_b, qg, kg, bd)


def _attn_kernel(q_ref, kt_ref, v_ref, g_ref, o_ref):
    lane = lax.broadcasted_iota(jnp.int32, (1, LANES), 1)
    low = lane < HEAD_DIM
    cols = []
    for c in range(ATTN_WIDTH // LANES):
        qp = q_ref[:, c * LANES:(c + 1) * LANES]
        acc = None
        inv = []
        for half in range(2):
            r = ((c // 2) * 2 + half) * LANES
            s = jnp.dot(qp, kt_ref[0, r:r + LANES, :], preferred_element_type=F32)
            m = jnp.max(s, axis=-1, keepdims=True)
            p = jnp.exp(s - m)
            inv.append(1.0 / jnp.sum(p, axis=-1, keepdims=True))
            pv = jnp.dot(p.astype(BF16), v_ref[:, r:r + LANES], preferred_element_type=F32)
            acc = pv if acc is None else acc + pv
        cols.append(acc * jnp.where(low, inv[0], inv[1]))
    o = jnp.concatenate(cols, axis=-1)
    o_ref[...] = _rms(o, g_ref[...]).astype(BF16)


def _attention(q, kt, v, gain, batch, seq):
    t = q.shape[0]
    nq = seq // TQ
    return pl.pallas_call(
        _attn_kernel,
        grid=(batch, nq),
        in_specs=[
            pl.BlockSpec((TQ, ATTN_WIDTH), lambda b, i: (b * nq + i, 0)),
            pl.BlockSpec((1, 4 * LANES, seq), lambda b, i: (b, 0, 0)),
            pl.BlockSpec((seq, 4 * LANES), lambda b, i: (b, 0)),
            pl.BlockSpec((1, ATTN_WIDTH), lambda b, i: (0, 0)),
        ],
        out_specs=pl.BlockSpec((TQ, ATTN_WIDTH), lambda b, i: (b * nq + i, 0)),
        out_shape=jax.ShapeDtypeStruct((t, ATTN_WIDTH), BF16),
        compiler_params=pltpu.CompilerParams(
            dimension_semantics=("arbitrary", "arbitrary"), vmem_limit_bytes=VMEM_LIMIT),
        name="attention",
    )(q, kt, v, gain)


def _lru_kernel(xl_ref, gl_ref, cw_ref, cb_ref, wg_ref, br_ref, bi_ref, lam_ref, g_ref,
                o_ref, xpad, a_f, b_f, a_b, b_b, cf_ref, cbk_ref):
    seq = xl_ref.shape[0]
    seg = seq // N_SEG
    half_w = LRU_WIDTH // 2
    pad = SUBLANES

    zeros_pad = jnp.zeros((pad, LRU_WIDTH), F32)
    xpad[0:pad, :] = zeros_pad
    xpad[pad + seq:pad + seq + pad, :] = zeros_pad
    xpad[pad:pad + seq, :] = xl_ref[...]

    lam = lam_ref[...]
    neg_c_sp = -LRU_C * (jnp.maximum(-lam, 0.0) + jnp.log1p(jnp.exp(-jnp.abs(lam))))
    cw = cw_ref[...]
    cbias = cb_ref[...]
    br = br_ref[...]
    bi = bi_ref[...]
    a_scr = (a_f, a_b)
    b_scr = (b_f, b_b)

    def gate_chunk(ci, _):
        r0 = pl.multiple_of(ci * GATE_ROWS, GATE_ROWS)
        w = xpad[pl.ds(r0, GATE_ROWS + 2 * pad), :]
        xc = cbias
        for j in range(CONV_WIDTH):
            xc = xc + w[pad - 1 + j:pad - 1 + j + GATE_ROWS, :] * cw[j:j + 1, :]
        xcb = xc.astype(BF16)
        for h in range(2):
            ls = slice(h * half_w, (h + 1) * half_w)
            z = jnp.dot(xcb[:, ls], wg_ref[h], preferred_element_type=F32)
            xh = xc[:, ls]
            for d in range(2):
                zr = z[:, (2 * d) * half_w:(2 * d + 1) * half_w] + br[d:d + 1, ls]
                zi = z[:, (2 * d + 1) * half_w:(2 * d + 2) * half_w] + bi[d:d + 1, ls]
                log_a = neg_c_sp[d:d + 1, ls] * jax.nn.sigmoid(zr)
                a = jnp.exp(log_a)
                inp = jnp.sqrt(-jnp.tanh(log_a) * (a * a + 1.0)) * (jax.nn.sigmoid(zi) * xh)
                for cc in range(half_w // LANES):
                    col = h * (half_w // LANES) + cc
                    a_scr[d][col, pl.ds(r0, GATE_ROWS), :] = a[:, cc * LANES:(cc + 1) * LANES]
                    b_scr[d][col, pl.ds(r0, GATE_ROWS), :] = inp[:, cc * LANES:(cc + 1) * LANES]
        return 0

    lax.fori_loop(0, seq // GATE_ROWS, gate_chunk, 0)

    n_col = LRU_WIDTH // LANES

    def scan_step(t, carry):
        hf, pf, hb, pb = carry
        idx_f = pl.ds(t, N_SEG, stride=seg)
        idx_b = pl.ds(seg - 1 - t, N_SEG, stride=seg)
        nhf, npf, nhb, npb = [], [], [], []
        for c in range(n_col):
            af = a_f[c, idx_f, :]
            h = af * hf[c] + b_f[c, idx_f, :]
            p = af * pf[c]
            b_f[c, idx_f, :] = h
            a_f[c, idx_f, :] = p
            nhf.append(h)
            npf.append(p)
            ab = a_b[c, idx_b, :]
            h = ab * hb[c] + b_b[c, idx_b, :]
            p = ab * pb[c]
            b_b[c, idx_b, :] = h
            a_b[c, idx_b, :] = p
            nhb.append(h)
            npb.append(p)
        return tuple(nhf), tuple(npf), tuple(nhb), tuple(npb)

    zero = tuple(jnp.zeros((N_SEG, LANES), F32) for _ in range(n_col))
    one = tuple(jnp.ones((N_SEG, LANES), F32) for _ in range(n_col))
    hf, pf, hb, pb = lax.fori_loop(0, seg, scan_step, (zero, one, zero, one), unroll=4)

    for c in range(n_col):
        cin = jnp.zeros((1, LANES), F32)
        for j in range(N_SEG):
            cf_ref[c, j:j + 1, :] = cin
            cin = hf[c][j:j + 1, :] + pf[c][j:j + 1, :] * cin
        cin = jnp.zeros((1, LANES), F32)
        for j in range(N_SEG - 1, -1, -1):
            cbk_ref[c, j:j + 1, :] = cin
            cin = hb[c][j:j + 1, :] + pb[c][j:j + 1, :] * cin

    gain = g_ref[...]

    def out_chunk(ci, _):
        r0 = pl.multiple_of(ci * OUT_ROWS, OUT_ROWS)
        rows = pl.ds(r0, OUT_ROWS)
        j = r0 // seg
        y = jnp.concatenate(
            [b_f[c, rows, :] + a_f[c, rows, :] * cf_ref[c, pl.ds(j, 1), :]
             + b_b[c, rows, :] + a_b[c, rows, :] * cbk_ref[c, pl.ds(j, 1), :]
             for c in range(n_col)], axis=-1)
        out = y * jax.nn.gelu(gl_ref[rows, :])
        o_ref[rows, :] = _rms(out, gain).astype(BF16)
        return 0

    lax.fori_loop(0, seq // OUT_ROWS, out_chunk, 0)


def _lru(xl, gl, conv_w, conv_b, wg, b_r, b_i, lam, gain, batch, seq):
    t = xl.shape[0]
    row = lambda b: (b, 0)
    const2 = lambda b: (0, 0)
    return pl.pallas_call(
        _lru_kernel,
        grid=(batch,),
        in_specs=[
            pl.BlockSpec((seq, LRU_WIDTH), row),
            pl.BlockSpec((seq, LRU_WIDTH), row),
            pl.BlockSpec((CONV_WIDTH, LRU_WIDTH), const2),
            pl.BlockSpec((1, LRU_WIDTH), const2),
            pl.BlockSpec((2, LRU_WIDTH // 2, 2 * LRU_WIDTH), lambda b: (0, 0, 0)),
            pl.BlockSpec((2, LRU_WIDTH), const2),
            pl.BlockSpec((2, LRU_WIDTH), const2),
            pl.BlockSpec((2, LRU_WIDTH), const2),
            pl.BlockSpec((1, LRU_WIDTH), const2),
        ],
        out_specs=pl.BlockSpec((seq, LRU_WIDTH), row),
        out_shape=jax.ShapeDtypeStruct((t, LRU_WIDTH), BF16),
        scratch_shapes=[
            pltpu.VMEM((seq + 2 * SUBLANES, LRU_WIDTH), F32),
            pltpu.VMEM((LRU_WIDTH // LANES, seq, LANES), F32),
            pltpu.VMEM((LRU_WIDTH // LANES, seq, LANES), F32),
            pltpu.VMEM((LRU_WIDTH // LANES, seq, LANES), F32),
            pltpu.VMEM((LRU_WIDTH // LANES, seq, LANES), F32),
            pltpu.VMEM((LRU_WIDTH // LANES, N_SEG, LANES), F32),
            pltpu.VMEM((LRU_WIDTH // LANES, N_SEG, LANES), F32),
        ],
        compiler_params=pltpu.CompilerParams(
            dimension_semantics=("arbitrary",), vmem_limit_bytes=VMEM_LIMIT),
        name="rglru",
    )(xl, gl, conv_w, conv_b, wg, b_r, b_i, lam, gain)


def _ffn_kernel(x_ref, at_ref, lr_ref, wo_ref, g_ref, wg_ref, wu_ref, wd_ref, o_ref, acc_ref):
    h1 = (x_ref[...]
          + jnp.dot(at_ref[...], wo_ref[0:ATTN_WIDTH, :], preferred_element_type=F32)
          + jnp.dot(lr_ref[...], wo_ref[ATTN_WIDTH:, :], preferred_element_type=F32))
    u = _rms(h1, g_ref[...]).astype(BF16)
    acc_ref[...] = h1

    def chunk(c, _):
        gate = jnp.dot(u, wg_ref[c], preferred_element_type=F32)
        up = jnp.dot(u, wu_ref[c], preferred_element_type=F32)
        ff = (jax.nn.silu(gate) * up).astype(BF16)
        acc_ref[...] += jnp.dot(ff, wd_ref[c], preferred_element_type=F32)
        return 0

    lax.fori_loop(0, wg_ref.shape[0], chunk, 0)
    o_ref[...] = acc_ref[...]


def _ffn(x2, attn_n, lru_n, w_out, norm_g, wg, wu, wd):
    t = x2.shape[0]
    tm = TM_PROJ
    nc = wg.shape[0]
    row = lambda i: (i, 0)
    const2 = lambda i: (0, 0)
    const3 = lambda i: (0, 0, 0)
    return pl.pallas_call(
        _ffn_kernel,
        grid=(t // tm,),
        in_specs=[
            pl.BlockSpec((tm, D_MODEL), row),
            pl.BlockSpec((tm, ATTN_WIDTH), row),
            pl.BlockSpec((tm, LRU_WIDTH), row),
            pl.BlockSpec((D_MODEL, D_MODEL), const2),
            pl.BlockSpec((1, D_MODEL), const2),
            pl.BlockSpec((nc, D_MODEL, FF_CHUNK), const3),
            pl.BlockSpec((nc, D_MODEL, FF_CHUNK), const3),
            pl.BlockSpec((nc, FF_CHUNK, D_MODEL), const3),
        ],
        out_specs=pl.BlockSpec((tm, D_MODEL), row),
        out_shape=jax.ShapeDtypeStruct((t, D_MODEL), F32),
        scratch_shapes=[pltpu.VMEM((tm, D_MODEL), F32)],
        compiler_params=pltpu.CompilerParams(
            dimension_semantics=("arbitrary",), vmem_limit_bytes=VMEM_LIMIT),
        name="outproj_ffn",
    )(x2, attn_n, lru_n, w_out, norm_g, wg, wu, wd)


def _rope_tables(seq):
    rows = seq // GRID_W
    row = jnp.repeat(jnp.arange(rows, dtype=F32), GRID_W)
    col = jnp.tile(jnp.arange(GRID_W, dtype=F32), rows)
    axis_dim = HEAD_DIM // 2
    inv = ROPE_THETA ** (-jnp.arange(0, axis_dim, 2, dtype=F32) / axis_dim)
    ang_r = row[:, None] * inv[None, :]
    ang_c = col[:, None] * inv[None, :]
    ang = jnp.concatenate([ang_r, ang_r, ang_c, ang_c], axis=-1)
    ang = jnp.concatenate([ang, ang], axis=-1)
    first = (jnp.arange(LANES) % (axis_dim)) < (axis_dim // 2)
    cos = jnp.cos(ang)
    sin = jnp.sin(ang)
    sin_a = jnp.where(first[None, :], -sin, 0.0)
    sin_b = jnp.where(first[None, :], 0.0, sin)
    return cos, sin_a, sin_b


def _block_diag_gate(w, half):
    per = (LRU_WIDTH // 2) // LRU_BLOCK
    out = jnp.zeros((LRU_WIDTH // 2, LRU_WIDTH // 2), w.dtype)
    for j in range(per):
        out = lax.dynamic_update_slice(out, w[half * per + j], (j * LRU_BLOCK, j * LRU_BLOCK))
    return out


def kernel(x, norm_mix, w_in, q_norm, k_norm, conv_w, conv_b, w_rgate, b_rgate, w_igate, b_igate, lru_lambda, out_norm_attn, out_norm_lru, w_out, norm_ffn, w_gate, w_up, w_down):
    batch, seq, _ = x.shape
    depth = norm_mix.shape[0]
    cos2, sin_a, sin_b = _rope_tables(seq)
    head_id = jnp.arange(LANES) // HEAD_DIM
    bd = (head_id[:, None] == head_id[None, :]).astype(BF16)
    bd = jnp.concatenate([bd, bd], axis=0)

    h = x.reshape(batch * seq, D_MODEL)
    for l in range(depth):
        qg = jnp.tile(q_norm[l] * (HEAD_DIM ** -0.5), 2)[None, :]
        kg = jnp.tile(k_norm[l], 2)[None, :]
        q, kt, v, xl, gl = _inproj(h, norm_mix[l][None, :], w_in[l].astype(BF16),
                                   cos2, sin_a, sin_b, qg, kg, bd, batch, seq)
        attn_n = _attention(q, kt, v, out_norm_attn[l][None, :], batch, seq)

        wg = jnp.stack([
            jnp.concatenate([_block_diag_gate(w_rgate[l, 0], hf), _block_diag_gate(w_igate[l, 0], hf),
                             _block_diag_gate(w_rgate[l, 1], hf), _block_diag_gate(w_igate[l, 1], hf)],
                            axis=1)
            for hf in range(2)]).astype(BF16)
        lru_n = _lru(xl, gl, conv_w[l], conv_b[l][None, :], wg, b_rgate[l], b_igate[l],
                     lru_lambda[l], out_norm_lru[l][None, :], batch, seq)

        d_ff = w_gate.shape[-1]
        nc = d_ff // FF_CHUNK
        wgc = w_gate[l].astype(BF16).reshape(D_MODEL, nc, FF_CHUNK).transpose(1, 0, 2)
        wuc = w_up[l].astype(BF16).reshape(D_MODEL, nc, FF_CHUNK).transpose(1, 0, 2)
        wdc = w_down[l].astype(BF16).reshape(nc, FF_CHUNK, D_MODEL)
        h = _ffn(h, attn_n, lru_n, w_out[l].astype(BF16), norm_ffn[l][None, :], wgc, wuc, wdc)
    return h.reshape(batch, seq, D_MODEL)
```

```python
import functools

import jax
import jax.numpy as jnp
from jax import lax
from jax.experimental import pallas as pl
from jax.experimental.pallas import tpu as pltpu

F32 = jnp.float32
BF16 = jnp.bfloat16

D_MODEL = 1024
GRID_W = 64
HEAD_DIM = 64
N_Q_HEADS = 8
N_KV_HEADS = 2
ATTN_WIDTH = N_Q_HEADS * HEAD_DIM
KV_WIDTH = N_KV_HEADS * HEAD_DIM
LRU_WIDTH = D_MODEL - ATTN_WIDTH
LRU_BLOCKS = 8
LRU_BLOCK = LRU_WIDTH // LRU_BLOCKS
CONV_WIDTH = 4
LRU_C = 8.0
D_IN = ATTN_WIDTH + 2 * KV_WIDTH + 2 * LRU_WIDTH
ROPE_THETA = 10000.0
EPS = 1e-6

LANES = 128
SUBLANES = 8
MXU_DIM = 256
VMEM_LIMIT = 56 * 1024 * 1024

TM_PROJ = 512
TQ = 256
FF_CHUNK = MXU_DIM
GATE_ROWS = 128
OUT_ROWS = 64
N_SEG = SUBLANES
SEG_SKEW = 1
LOG2E = 1.4426950408889634


def _rms(x, gain):
    return x * lax.rsqrt(jnp.mean(x * x, axis=-1, keepdims=True) + EPS) * gain


def _inproj_kernel(x_ref, g_ref, w_ref, cos_ref, sa_ref, sb_ref, qg_ref, kg_ref, bd_ref,
                   q_ref, kt_ref, v_ref, xl_ref, gl_ref):
    u = _rms(x_ref[...], g_ref[...]).astype(BF16)
    cos = cos_ref[...]
    sin_a = sa_ref[...]
    sin_b = sb_ref[...]
    bd = bd_ref[...]
    lane = lax.broadcasted_iota(jnp.int32, (1, LANES), 1)
    low = lane < HEAD_DIM

    def norm_rope(z, gain):
        sq = z * z
        hi = sq.astype(BF16)
        lo = (sq - hi.astype(F32)).astype(BF16)
        ss = jnp.dot(jnp.concatenate([hi, lo], axis=-1), bd, preferred_element_type=F32)
        y = z * lax.rsqrt(ss * (1.0 / HEAD_DIM) + EPS) * gain
        return (y * cos + pltpu.roll(y, LANES - 16, 1) * sin_a
                + pltpu.roll(y, 16, 1) * sin_b)

    qg = qg_ref[...]
    zq = jnp.dot(u, w_ref[:, 0:ATTN_WIDTH], preferred_element_type=F32)
    for c in range(ATTN_WIDTH // LANES):
        q_ref[:, c * LANES:(c + 1) * LANES] = norm_rope(
            zq[:, c * LANES:(c + 1) * LANES], qg).astype(BF16)

    o = ATTN_WIDTH
    zkv = jnp.dot(u, w_ref[:, o:o + 2 * KV_WIDTH], preferred_element_type=F32)
    k2 = norm_rope(zkv[:, 0:KV_WIDTH], kg_ref[...])
    v2 = zkv[:, KV_WIDTH:2 * KV_WIDTH]
    o += 2 * KV_WIDTH

    def expand(z):
        a0 = jnp.where(low, z, 0.0)
        b1 = jnp.where(low, 0.0, z)
        return jnp.concatenate([a0, pltpu.roll(a0, HEAD_DIM, 1), pltpu.roll(b1, HEAD_DIM, 1), b1], axis=-1)

    kt_ref[0] = expand(k2).T.astype(BF16)
    v_ref[...] = expand(v2).astype(BF16)
    xl_ref[...] = jnp.dot(u, w_ref[:, o:o + LRU_WIDTH], preferred_element_type=F32)
    o += LRU_WIDTH
    gl_ref[...] = jnp.dot(u, w_ref[:, o:o + LRU_WIDTH], preferred_element_type=F32)


def _inproj(x2, norm_g, w_in, cos2, sin_a, sin_b, qg, kg, bd, batch, seq):
    t = x2.shape[0]
    tm = TM_PROJ
    per_seq = seq // tm
    row = lambda i: (i, 0)
    const = lambda i: (0, 0)
    pos = lambda i: (i % per_seq, 0)
    return pl.pallas_call(
        _inproj_kernel,
        grid=(t // tm,),
        in_specs=[
            pl.BlockSpec((tm, D_MODEL), row),
            pl.BlockSpec((1, D_MODEL), const),
            pl.BlockSpec((D_MODEL, D_IN), const),
            pl.BlockSpec((tm, LANES), pos),
            pl.BlockSpec((tm, LANES), pos),
            pl.BlockSpec((tm, LANES), pos),
            pl.BlockSpec((1, LANES), const),
            pl.BlockSpec((1, LANES), const),
            pl.BlockSpec((2 * LANES, LANES), const),
        ],
        out_specs=[
            pl.BlockSpec((tm, ATTN_WIDTH), row),
            pl.BlockSpec((1, 4 * LANES, tm), lambda i: (i // per_seq, 0, i % per_seq)),
            pl.BlockSpec((tm, 4 * LANES), row),
            pl.BlockSpec((tm, LRU_WIDTH), row),
            pl.BlockSpec((tm, LRU_WIDTH), row),
        ],
        out_shape=[
            jax.ShapeDtypeStruct((t, ATTN_WIDTH), BF16),
            jax.ShapeDtypeStruct((batch, 4 * LANES, seq), BF16),
            jax.ShapeDtypeStruct((t, 4 * LANES), BF16),
            jax.ShapeDtypeStruct((t, LRU_WIDTH), F32),
            jax.ShapeDtypeStruct((t, LRU_WIDTH), F32),
        ],
        compiler_params=pltpu.CompilerParams(
            dimension_semantics=("arbitrary",), vmem_limit_bytes=VMEM_LIMIT),
        name="inproj",
    )(x2, norm_g, w_in, cos2, sin_a, sin_b, qg, kg, bd)


def _attn_kernel(q_ref, kt_ref, v_ref, g_ref, o_ref):
    lane = lax.broadcasted_iota(jnp.int32, (1, LANES), 1)
    low = lane < HEAD_DIM
    cols = []
    for c in range(ATTN_WIDTH // LANES):
        qp = q_ref[:, c * LANES:(c + 1) * LANES]
        acc = None
        inv = []
        for half in range(2):
            r = ((c // 2) * 2 + half) * LANES
            s = jnp.dot(qp, kt_ref[0, r:r + LANES, :], preferred_element_type=F32)
            m = jnp.max(s, axis=-1, keepdims=True)
            p = jnp.exp(s - m)
            inv.append(1.0 / jnp.sum(p, axis=-1, keepdims=True))
            pv = jnp.dot(p.astype(BF16), v_ref[:, r:r + LANES], preferred_element_type=F32)
            acc = pv if acc is None else acc + pv
        cols.append(acc * jnp.where(low, inv[0], inv[1]))
    o = jnp.concatenate(cols, axis=-1)
    o_ref[...] = _rms(o, g_ref[...]).astype(BF16)


def _attention(q, kt, v, gain, batch, seq):
    t = q.shape[0]
    nq = seq // TQ
    return pl.pallas_call(
        _attn_kernel,
        grid=(batch, nq),
        in_specs=[
            pl.BlockSpec((TQ, ATTN_WIDTH), lambda b, i: (b * nq + i, 0)),
            pl.BlockSpec((1, 4 * LANES, seq), lambda b, i: (b, 0, 0)),
            pl.BlockSpec((seq, 4 * LANES), lambda b, i: (b, 0)),
            pl.BlockSpec((1, ATTN_WIDTH), lambda b, i: (0, 0)),
        ],
        out_specs=pl.BlockSpec((TQ, ATTN_WIDTH), lambda b, i: (b * nq + i, 0)),
        out_shape=jax.ShapeDtypeStruct((t, ATTN_WIDTH), BF16),
        compiler_params=pltpu.CompilerParams(
            dimension_semantics=("arbitrary", "arbitrary"), vmem_limit_bytes=VMEM_LIMIT),
        name="attention",
    )(q, kt, v, gain)


def _lru_kernel(xl_ref, gl_ref, cw_ref, cb_ref, wg_ref, br_ref, bi_ref, lam_ref, g_ref,
                o_ref, xcol, a_f, b_f, a_b, b_b, cf_ref, cbk_ref):
    seq = xl_ref.shape[0]
    seg = seq // N_SEG
    sst = seg + SEG_SKEW
    n_col = LRU_WIDTH // LANES
    half_cols = n_col // 2
    half_w = LRU_WIDTH // 2
    pad = SUBLANES

    zeros_pad = jnp.zeros((pad, LANES), F32)
    for c in range(n_col):
        xcol[c, 0:pad, :] = zeros_pad
        xcol[c, pad + seq:pad + seq + pad, :] = zeros_pad
        xcol[c, pad:pad + seq, :] = xl_ref[:, c * LANES:(c + 1) * LANES]

    lam = lam_ref[...]
    k1 = (0.5 * LRU_C) * (jnp.maximum(-lam, 0.0) + jnp.log1p(jnp.exp(-jnp.abs(lam))))
    cw = cw_ref[...]
    cbias = cb_ref[...]
    half_br = 0.5 * br_ref[...]
    half_bi = 0.5 * bi_ref[...]
    a_scr = (a_f, a_b)
    b_scr = (b_f, b_b)

    def gate_chunk(ci, _):
        r0 = pl.multiple_of(ci * GATE_ROWS, GATE_ROWS)
        dst = pl.ds(r0 + (r0 // seg) * SEG_SKEW, GATE_ROWS)
        xc = []
        for c in range(n_col):
            cs = slice(c * LANES, (c + 1) * LANES)
            acc = cbias[:, cs]
            for k in range(CONV_WIDTH):
                acc = acc + xcol[c, pl.ds(r0 + pad - 1 + k, GATE_ROWS), :] * cw[k:k + 1, cs]
            xc.append(acc)
        for h in range(2):
            lhs = jnp.concatenate(xc[h * half_cols:(h + 1) * half_cols], axis=-1).astype(BF16)
            z = jnp.dot(lhs, wg_ref[h], preferred_element_type=F32)
            for cc in range(half_cols):
                c = h * half_cols + cc
                cs = slice(c * LANES, (c + 1) * LANES)
                half_x = 0.5 * xc[c]
                for d in range(2):
                    zo = 2 * d * half_w + cc * LANES
                    zr = z[:, zo:zo + LANES] + half_br[d:d + 1, cs]
                    zi = z[:, zo + half_w:zo + half_w + LANES] + half_bi[d:d + 1, cs]
                    k1d = k1[d:d + 1, cs]
                    w = k1d * jnp.tanh(zr) + k1d
                    a = jnp.exp2(w * (-LOG2E))
                    v = jnp.tanh(w) * (a * a + 1.0)
                    mult = jnp.where(v > 0.0, v * lax.rsqrt(v), 0.0)
                    a_scr[d][c, dst, :] = a
                    b_scr[d][c, dst, :] = mult * ((jnp.tanh(zi) + 1.0) * half_x)
        return 0

    lax.fori_loop(0, seq // GATE_ROWS, gate_chunk, 0)

    def scan_step(t, carry):
        hf, pf, hb, pb = carry
        idx_f = pl.ds(t, N_SEG, stride=sst)
        idx_b = pl.ds(seg - 1 - t, N_SEG, stride=sst)
        nhf, npf, nhb, npb = [], [], [], []
        for c in range(n_col):
            af = a_f[c, idx_f, :]
            h = af * hf[c] + b_f[c, idx_f, :]
            p = af * pf[c]
            b_f[c, idx_f, :] = h
            a_f[c, idx_f, :] = p
            nhf.append(h)
            npf.append(p)
            ab = a_b[c, idx_b, :]
            h = ab * hb[c] + b_b[c, idx_b, :]
            p = ab * pb[c]
            b_b[c, idx_b, :] = h
            a_b[c, idx_b, :] = p
            nhb.append(h)
            npb.append(p)
        return tuple(nhf), tuple(npf), tuple(nhb), tuple(npb)

    zero = tuple(jnp.zeros((N_SEG, LANES), F32) for _ in range(n_col))
    one = tuple(jnp.ones((N_SEG, LANES), F32) for _ in range(n_col))
    hf, pf, hb, pb = lax.fori_loop(0, seg, scan_step, (zero, one, zero, one), unroll=4)

    for c in range(n_col):
        cin = jnp.zeros((1, LANES), F32)
        for j in range(N_SEG):
            cf_ref[c, j:j + 1, :] = cin
            cin = hf[c][j:j + 1, :] + pf[c][j:j + 1, :] * cin
        cin = jnp.zeros((1, LANES), F32)
        for j in range(N_SEG - 1, -1, -1):
            cbk_ref[c, j:j + 1, :] = cin
            cin = hb[c][j:j + 1, :] + pb[c][j:j + 1, :] * cin

    gain = g_ref[...]

    def out_chunk(ci, _):
        r0 = pl.multiple_of(ci * OUT_ROWS, OUT_ROWS)
        rows = pl.ds(r0, OUT_ROWS)
        j = r0 // seg
        src = pl.ds(r0 + j * SEG_SKEW, OUT_ROWS)
        y = jnp.concatenate(
            [b_f[c, src, :] + a_f[c, src, :] * cf_ref[c, pl.ds(j, 1), :]
             + b_b[c, src, :] + a_b[c, src, :] * cbk_ref[c, pl.ds(j, 1), :]
             for c in range(n_col)], axis=-1)
        out = y * jax.nn.gelu(gl_ref[rows, :])
        o_ref[rows, :] = _rms(out, gain).astype(BF16)
        return 0

    lax.fori_loop(0, seq // OUT_ROWS, out_chunk, 0)


def _lru(xl, gl, conv_w, conv_b, wg, b_r, b_i, lam, gain, batch, seq):
    t = xl.shape[0]
    row = lambda b: (b, 0)
    const2 = lambda b: (0, 0)
    return pl.pallas_call(
        _lru_kernel,
        grid=(batch,),
        in_specs=[
            pl.BlockSpec((seq, LRU_WIDTH), row),
            pl.BlockSpec((seq, LRU_WIDTH), row),
            pl.BlockSpec((CONV_WIDTH, LRU_WIDTH), const2),
            pl.BlockSpec((1, LRU_WIDTH), const2),
            pl.BlockSpec((2, LRU_WIDTH // 2, 2 * LRU_WIDTH), lambda b: (0, 0, 0)),
            pl.BlockSpec((2, LRU_WIDTH), const2),
            pl.BlockSpec((2, LRU_WIDTH), const2),
            pl.BlockSpec((2, LRU_WIDTH), const2),
            pl.BlockSpec((1, LRU_WIDTH), const2),
        ],
        out_specs=pl.BlockSpec((seq, LRU_WIDTH), row),
        out_shape=jax.ShapeDtypeStruct((t, LRU_WIDTH), BF16),
        scratch_shapes=[
            pltpu.VMEM((LRU_WIDTH // LANES, seq + 2 * SUBLANES, LANES), F32),
            pltpu.VMEM((LRU_WIDTH // LANES, seq + N_SEG * SEG_SKEW, LANES), F32),
            pltpu.VMEM((LRU_WIDTH // LANES, seq + N_SEG * SEG_SKEW, LANES), F32),
            pltpu.VMEM((LRU_WIDTH // LANES, seq + N_SEG * SEG_SKEW, LANES), F32),
            pltpu.VMEM((LRU_WIDTH // LANES, seq + N_SEG * SEG_SKEW, LANES), F32),
            pltpu.VMEM((LRU_WIDTH // LANES, N_SEG, LANES), F32),
            pltpu.VMEM((LRU_WIDTH // LANES, N_SEG, LANES), F32),
        ],
        compiler_params=pltpu.CompilerParams(
            dimension_semantics=("arbitrary",), vmem_limit_bytes=VMEM_LIMIT),
        name="rglru",
    )(xl, gl, conv_w, conv_b, wg, b_r, b_i, lam, gain)


def _ffn_kernel(x_ref, at_ref, lr_ref, wo_ref, g_ref, wg_ref, wu_ref, wd_ref, o_ref, ff_ref):
    h1 = (x_ref[...]
          + jnp.dot(at_ref[...], wo_ref[0:ATTN_WIDTH, :], preferred_element_type=F32)
          + jnp.dot(lr_ref[...], wo_ref[ATTN_WIDTH:, :], preferred_element_type=F32))
    u = _rms(h1, g_ref[...]).astype(BF16)
    o_ref[...] = h1

    for c in range(wg_ref.shape[0]):
        gate = jnp.dot(u, wg_ref[c], preferred_element_type=F32)
        up = jnp.dot(u, wu_ref[c], preferred_element_type=F32)
        ff_ref[:, c * FF_CHUNK:(c + 1) * FF_CHUNK] = (jax.nn.silu(gate) * up).astype(BF16)

    o_ref[...] += jnp.dot(ff_ref[...], wd_ref[...], preferred_element_type=F32)


def _ffn(x2, attn_n, lru_n, w_out, norm_g, wg, wu, wd):
    t = x2.shape[0]
    tm = TM_PROJ
    nc = wg.shape[0]
    row = lambda i: (i, 0)
    const2 = lambda i: (0, 0)
    const3 = lambda i: (0, 0, 0)
    return pl.pallas_call(
        _ffn_kernel,
        grid=(t // tm,),
        in_specs=[
            pl.BlockSpec((tm, D_MODEL), row),
            pl.BlockSpec((tm, ATTN_WIDTH), row),
            pl.BlockSpec((tm, LRU_WIDTH), row),
            pl.BlockSpec((D_MODEL, D_MODEL), const2),
            pl.BlockSpec((1, D_MODEL), const2),
            pl.BlockSpec((nc, D_MODEL, FF_CHUNK), const3),
            pl.BlockSpec((nc, D_MODEL, FF_CHUNK), const3),
            pl.BlockSpec((nc * FF_CHUNK, D_MODEL), const2),
        ],
        out_specs=pl.BlockSpec((tm, D_MODEL), row),
        out_shape=jax.ShapeDtypeStruct((t, D_MODEL), F32),
        scratch_shapes=[pltpu.VMEM((tm, nc * FF_CHUNK), BF16)],
        compiler_params=pltpu.CompilerParams(
            dimension_semantics=("arbitrary",), vmem_limit_bytes=VMEM_LIMIT),
        name="outproj_ffn",
    )(x2, attn_n, lru_n, w_out, norm_g, wg, wu, wd)


def _rope_tables(seq):
    rows = seq // GRID_W
    row = jnp.repeat(jnp.arange(rows, dtype=F32), GRID_W)
    col = jnp.tile(jnp.arange(GRID_W, dtype=F32), rows)
    axis_dim = HEAD_DIM // 2
    inv = ROPE_THETA ** (-jnp.arange(0, axis_dim, 2, dtype=F32) / axis_dim)
    ang_r = row[:, None] * inv[None, :]
    ang_c = col[:, None] * inv[None, :]
    ang = jnp.concatenate([ang_r, ang_r, ang_c, ang_c], axis=-1)
    ang = jnp.concatenate([ang, ang], axis=-1)
    first = (jnp.arange(LANES) % (axis_dim)) < (axis_dim // 2)
    cos = jnp.cos(ang)
    sin = jnp.sin(ang)
    sin_a = jnp.where(first[None, :], -sin, 0.0)
    sin_b = jnp.where(first[None, :], 0.0, sin)
    return cos, sin_a, sin_b


def _block_diag_gate(w, half):
    per = (LRU_WIDTH // 2) // LRU_BLOCK
    out = jnp.zeros((LRU_WIDTH // 2, LRU_WIDTH // 2), w.dtype)
    for j in range(per):
        out = lax.dynamic_update_slice(out, w[half * per + j], (j * LRU_BLOCK, j * LRU_BLOCK))
    return out


def kernel(x, norm_mix, w_in, q_norm, k_norm, conv_w, conv_b, w_rgate, b_rgate, w_igate, b_igate, lru_lambda, out_norm_attn, out_norm_lru, w_out, norm_ffn, w_gate, w_up, w_down):
    batch, seq, _ = x.shape
    depth = norm_mix.shape[0]
    cos2, sin_a, sin_b = _rope_tables(seq)
    head_id = jnp.arange(LANES) // HEAD_DIM
    bd = (head_id[:, None] == head_id[None, :]).astype(BF16)
    bd = jnp.concatenate([bd, bd], axis=0)

    h = x.reshape(batch * seq, D_MODEL)
    for l in range(depth):
        qg = jnp.tile(q_norm[l] * (HEAD_DIM ** -0.5), 2)[None, :]
        kg = jnp.tile(k_norm[l], 2)[None, :]
        q, kt, v, xl, gl = _inproj(h, norm_mix[l][None, :], w_in[l].astype(BF16),
                                   cos2, sin_a, sin_b, qg, kg, bd, batch, seq)
        attn_n = _attention(q, kt, v, out_norm_attn[l][None, :], batch, seq)

        wg = jnp.stack([
            jnp.concatenate([_block_diag_gate(w_rgate[l, 0], hf), _block_diag_gate(w_igate[l, 0], hf),
                             _block_diag_gate(w_rgate[l, 1], hf), _block_diag_gate(w_igate[l, 1], hf)],
                            axis=1)
            for hf in range(2)])
        wg = (0.5 * wg).astype(BF16)
        lru_n = _lru(xl, gl, conv_w[l], conv_b[l][None, :], wg, b_rgate[l], b_igate[l],
                     lru_lambda[l], out_norm_lru[l][None, :], batch, seq)

        d_ff = w_gate.shape[-1]
        nc = d_ff // FF_CHUNK
        wgc = w_gate[l].astype(BF16).reshape(D_MODEL, nc, FF_CHUNK).transpose(1, 0, 2)
        wuc = w_up[l].astype(BF16).reshape(D_MODEL, nc, FF_CHUNK).transpose(1, 0, 2)
        h = _ffn(h, attn_n, lru_n, w_out[l].astype(BF16), norm_ffn[l][None, :], wgc, wuc,
                 w_down[l].astype(BF16))
    return h.reshape(batch, seq, D_MODEL)
```

```python
import functools

import jax
import jax.numpy as jnp
from jax import lax
from jax.experimental import pallas as pl
from jax.experimental.pallas import tpu as pltpu

F32 = jnp.float32
BF16 = jnp.bfloat16

D_MODEL = 1024
GRID_W = 64
HEAD_DIM = 64
N_Q_HEADS = 8
N_KV_HEADS = 2
ATTN_WIDTH = N_Q_HEADS * HEAD_DIM
KV_WIDTH = N_KV_HEADS * HEAD_DIM
LRU_WIDTH = D_MODEL - ATTN_WIDTH
LRU_BLOCKS = 8
LRU_BLOCK = LRU_WIDTH // LRU_BLOCKS
CONV_WIDTH = 4
LRU_C = 8.0
D_IN = ATTN_WIDTH + 2 * KV_WIDTH + 2 * LRU_WIDTH
ROPE_THETA = 10000.0
EPS = 1e-6

LANES = 128
SUBLANES = 8
MXU_DIM = 256
VMEM_LIMIT = 56 * 1024 * 1024

TM_PROJ = 512
TQ = 512
KEY_CHUNK = 2048
FF_CHUNK = MXU_DIM
GATE_ROWS = 128
OUT_ROWS = 64
N_SEG = SUBLANES
SEG_SKEW = 1
LOG2E = 1.4426950408889634
V_ROWS = HEAD_DIM + 16


def _rms(x, gain):
    return x * lax.rsqrt(jnp.mean(x * x, axis=-1, keepdims=True) + EPS) * gain


def _inproj_kernel(x_ref, g_ref, w_ref, cos_ref, sa_ref, sb_ref, qg_ref, kg_ref, bd_ref,
                   q_ref, k_ref, vt_ref, xl_ref, gl_ref):
    u = _rms(x_ref[...], g_ref[...]).astype(BF16)
    cos = cos_ref[...]
    sin_a = sa_ref[...]
    sin_b = sb_ref[...]
    bd = bd_ref[...]
    lane = lax.broadcasted_iota(jnp.int32, (1, LANES), 1)
    low = lane < HEAD_DIM

    def norm_rope(z, gain):
        sq = z * z
        hi = sq.astype(BF16)
        lo = (sq - hi.astype(F32)).astype(BF16)
        ss = jnp.dot(jnp.concatenate([hi, lo], axis=-1), bd, preferred_element_type=F32)
        y = z * lax.rsqrt(ss * (1.0 / HEAD_DIM) + EPS) * gain
        return (y * cos + pltpu.roll(y, LANES - 16, 1) * sin_a
                + pltpu.roll(y, 16, 1) * sin_b)

    qg = qg_ref[...]
    zq = jnp.dot(u, w_ref[:, 0:ATTN_WIDTH], preferred_element_type=F32)
    for c in range(ATTN_WIDTH // LANES):
        q_ref[:, c * LANES:(c + 1) * LANES] = norm_rope(
            zq[:, c * LANES:(c + 1) * LANES], qg).astype(BF16)

    o = ATTN_WIDTH
    zkv = jnp.dot(u, w_ref[:, o:o + 2 * KV_WIDTH], preferred_element_type=F32)
    k2 = norm_rope(zkv[:, 0:KV_WIDTH], kg_ref[...])
    v2 = zkv[:, KV_WIDTH:2 * KV_WIDTH]
    o += 2 * KV_WIDTH

    a0 = jnp.where(low, k2, 0.0)
    b1 = jnp.where(low, 0.0, k2)
    k_ref[...] = jnp.concatenate(
        [a0, pltpu.roll(a0, HEAD_DIM, 1), pltpu.roll(b1, HEAD_DIM, 1), b1], axis=-1).astype(BF16)

    vt = v2.T.astype(BF16)
    ones = jnp.ones((V_ROWS - HEAD_DIM, vt.shape[1]), BF16)
    for g in range(N_KV_HEADS):
        vt_ref[0, g, 0:HEAD_DIM, :] = vt[g * HEAD_DIM:(g + 1) * HEAD_DIM, :]
        vt_ref[0, g, HEAD_DIM:V_ROWS, :] = ones
    xl_ref[...] = jnp.dot(u, w_ref[:, o:o + LRU_WIDTH], preferred_element_type=F32)
    o += LRU_WIDTH
    gl_ref[...] = jnp.dot(u, w_ref[:, o:o + LRU_WIDTH], preferred_element_type=F32)


def _inproj(x2, norm_g, w_in, cos2, sin_a, sin_b, qg, kg, bd, batch, seq):
    t = x2.shape[0]
    tm = TM_PROJ
    per_seq = seq // tm
    row = lambda i: (i, 0)
    const = lambda i: (0, 0)
    pos = lambda i: (i % per_seq, 0)
    return pl.pallas_call(
        _inproj_kernel,
        grid=(t // tm,),
        in_specs=[
            pl.BlockSpec((tm, D_MODEL), row),
            pl.BlockSpec((1, D_MODEL), const),
            pl.BlockSpec((D_MODEL, D_IN), const),
            pl.BlockSpec((tm, LANES), pos),
            pl.BlockSpec((tm, LANES), pos),
            pl.BlockSpec((tm, LANES), pos),
            pl.BlockSpec((1, LANES), const),
            pl.BlockSpec((1, LANES), const),
            pl.BlockSpec((2 * LANES, LANES), const),
        ],
        out_specs=[
            pl.BlockSpec((tm, ATTN_WIDTH), row),
            pl.BlockSpec((tm, 4 * LANES), row),
            pl.BlockSpec((1, N_KV_HEADS, V_ROWS, tm), lambda i: (i // per_seq, 0, 0, i % per_seq)),
            pl.BlockSpec((tm, LRU_WIDTH), row),
            pl.BlockSpec((tm, LRU_WIDTH), row),
        ],
        out_shape=[
            jax.ShapeDtypeStruct((t, ATTN_WIDTH), BF16),
            jax.ShapeDtypeStruct((t, 4 * LANES), BF16),
            jax.ShapeDtypeStruct((batch, N_KV_HEADS, V_ROWS, seq), BF16),
            jax.ShapeDtypeStruct((t, LRU_WIDTH), F32),
            jax.ShapeDtypeStruct((t, LRU_WIDTH), F32),
        ],
        compiler_params=pltpu.CompilerParams(
            dimension_semantics=("arbitrary",), vmem_limit_bytes=VMEM_LIMIT),
        name="inproj",
    )(x2, norm_g, w_in, cos2, sin_a, sin_b, qg, kg, bd)


def _attn_kernel(q_ref, k_ref, vt_ref, g_ref, o_ref, ot_ref, s0_ref, s1_ref):
    s_slots = (s0_ref, s1_ref)
    nt = (((1,), (1,)), ((), ()))
    group = N_Q_HEADS // N_KV_HEADS
    n_chunk = k_ref.shape[0] // KEY_CHUNK

    def qk_chunk(h, c):
        col = (2 * (h // group) + h % 2) * LANES
        rows = slice(c * KEY_CHUNK, (c + 1) * KEY_CHUNK)
        s = lax.dot_general(k_ref[rows, col:col + LANES], q_ref[:, (h // 2) * LANES:(h // 2 + 1) * LANES],
                            nt, preferred_element_type=F32)
        s_slots[h % 2][rows, :] = s
        return jnp.max(s, axis=0, keepdims=True)

    def pv_chunk(h, c, m):
        rows = slice(c * KEY_CHUNK, (c + 1) * KEY_CHUNK)
        p = jnp.exp2(s_slots[h % 2][rows, :] - m).astype(BF16)
        return jnp.dot(vt_ref[0, h // group, :, rows], p, preferred_element_type=F32)

    def running_max(m, m_c):
        return m_c if m is None else jnp.maximum(m, m_c)

    m_cur = None
    for c in range(n_chunk):
        m_cur = running_max(m_cur, qk_chunk(0, c))
    for h in range(N_Q_HEADS):
        m_next = None
        acc = None
        for c in range(n_chunk):
            if h + 1 < N_Q_HEADS:
                m_next = running_max(m_next, qk_chunk(h + 1, c))
            pv = pv_chunk(h, c, m_cur)
            acc = pv if acc is None else acc + pv
        inv = 1.0 / acc[HEAD_DIM:HEAD_DIM + 1, :]
        ot_ref[h * HEAD_DIM:(h + 1) * HEAD_DIM, :] = acc[0:HEAD_DIM, :] * inv
        m_cur = m_next
    o_ref[...] = _rms(ot_ref[...].T, g_ref[...]).astype(BF16)


def _attention(q, k, vt, gain, batch, seq):
    t = q.shape[0]
    nq = seq // TQ
    return pl.pallas_call(
        _attn_kernel,
        grid=(batch, nq),
        in_specs=[
            pl.BlockSpec((TQ, ATTN_WIDTH), lambda b, i: (b * nq + i, 0)),
            pl.BlockSpec((seq, 4 * LANES), lambda b, i: (b, 0)),
            pl.BlockSpec((1, N_KV_HEADS, V_ROWS, seq), lambda b, i: (b, 0, 0, 0)),
            pl.BlockSpec((1, ATTN_WIDTH), lambda b, i: (0, 0)),
        ],
        out_specs=pl.BlockSpec((TQ, ATTN_WIDTH), lambda b, i: (b * nq + i, 0)),
        out_shape=jax.ShapeDtypeStruct((t, ATTN_WIDTH), BF16),
        scratch_shapes=[pltpu.VMEM((ATTN_WIDTH, TQ), F32), pltpu.VMEM((seq, TQ), F32),
                        pltpu.VMEM((seq, TQ), F32)],
        compiler_params=pltpu.CompilerParams(
            dimension_semantics=("arbitrary", "arbitrary"), vmem_limit_bytes=VMEM_LIMIT),
        name="attention",
    )(q, k, vt, gain)


def _lru_kernel(xl_ref, gl_ref, cw_ref, cb_ref, wg_ref, br_ref, bi_ref, lam_ref, g_ref,
                o_ref, xcol, a_f, b_f, a_b, b_b, cf_ref, cbk_ref):
    seq = xl_ref.shape[0]
    seg = seq // N_SEG
    sst = seg + SEG_SKEW
    n_col = LRU_WIDTH // LANES
    half_cols = n_col // 2
    half_w = LRU_WIDTH // 2
    pad = SUBLANES

    zeros_pad = jnp.zeros((pad, LANES), F32)
    for c in range(n_col):
        xcol[c, 0:pad, :] = zeros_pad
        xcol[c, pad + seq:pad + seq + pad, :] = zeros_pad
        xcol[c, pad:pad + seq, :] = xl_ref[:, c * LANES:(c + 1) * LANES]

    lam = lam_ref[...]
    k1 = (0.5 * LRU_C) * (jnp.maximum(-lam, 0.0) + jnp.log1p(jnp.exp(-jnp.abs(lam))))
    cw = cw_ref[...]
    cbias = cb_ref[...]
    half_br = 0.5 * br_ref[...]
    half_bi = 0.5 * bi_ref[...]
    a_scr = (a_f, a_b)
    b_scr = (b_f, b_b)

    def gate_chunk(ci, _):
        r0 = pl.multiple_of(ci * GATE_ROWS, GATE_ROWS)
        dst = pl.ds(r0 + (r0 // seg) * SEG_SKEW, GATE_ROWS)
        xc = []
        for c in range(n_col):
            cs = slice(c * LANES, (c + 1) * LANES)
            acc = cbias[:, cs]
            for k in range(CONV_WIDTH):
                acc = acc + xcol[c, pl.ds(r0 + pad - 1 + k, GATE_ROWS), :] * cw[k:k + 1, cs]
            xc.append(acc)
        for h in range(2):
            lhs = jnp.concatenate(xc[h * half_cols:(h + 1) * half_cols], axis=-1).astype(BF16)
            z = jnp.dot(lhs, wg_ref[h], preferred_element_type=F32)
            for cc in range(half_cols):
                c = h * half_cols + cc
                cs = slice(c * LANES, (c + 1) * LANES)
                half_x = 0.5 * xc[c]
                for d in range(2):
                    zo = 2 * d * half_w + cc * LANES
                    zr = z[:, zo:zo + LANES] + half_br[d:d + 1, cs]
                    zi = z[:, zo + half_w:zo + half_w + LANES] + half_bi[d:d + 1, cs]
                    k1d = k1[d:d + 1, cs]
                    w = k1d * jnp.tanh(zr) + k1d
                    a = jnp.exp2(w * (-LOG2E))
                    v = jnp.tanh(w) * (a * a + 1.0)
                    mult = jnp.where(v > 0.0, v * lax.rsqrt(v), 0.0)
                    a_scr[d][c, dst, :] = a
                    b_scr[d][c, dst, :] = mult * ((jnp.tanh(zi) + 1.0) * half_x)
        return 0

    lax.fori_loop(0, seq // GATE_ROWS, gate_chunk, 0)

    def scan_step(t, carry):
        hf, pf, hb, pb = carry
        idx_f = pl.ds(t, N_SEG, stride=sst)
        idx_b = pl.ds(seg - 1 - t, N_SEG, stride=sst)
        nhf, npf, nhb, npb = [], [], [], []
        for c in range(n_col):
            af = a_f[c, idx_f, :]
            h = af * hf[c] + b_f[c, idx_f, :]
            p = af * pf[c]
            b_f[c, idx_f, :] = h
            a_f[c, idx_f, :] = p
            nhf.append(h)
            npf.append(p)
            ab = a_b[c, idx_b, :]
            h = ab * hb[c] + b_b[c, idx_b, :]
            p = ab * pb[c]
            b_b[c, idx_b, :] = h
            a_b[c, idx_b, :] = p
            nhb.append(h)
            npb.append(p)
        return tuple(nhf), tuple(npf), tuple(nhb), tuple(npb)

    zero = tuple(jnp.zeros((N_SEG, LANES), F32) for _ in range(n_col))
    one = tuple(jnp.ones((N_SEG, LANES), F32) for _ in range(n_col))
    hf, pf, hb, pb = lax.fori_loop(0, seg, scan_step, (zero, one, zero, one), unroll=4)

    for c in range(n_col):
        cin = jnp.zeros((1, LANES), F32)
        for j in range(N_SEG):
            cf_ref[c, j:j + 1, :] = cin
            cin = hf[c][j:j + 1, :] + pf[c][j:j + 1, :] * cin
        cin = jnp.zeros((1, LANES), F32)
        for j in range(N_SEG - 1, -1, -1):
            cbk_ref[c, j:j + 1, :] = cin
            cin = hb[c][j:j + 1, :] + pb[c][j:j + 1, :] * cin

    gain = g_ref[...]

    def out_chunk(ci, _):
        r0 = pl.multiple_of(ci * OUT_ROWS, OUT_ROWS)
        rows = pl.ds(r0, OUT_ROWS)
        j = r0 // seg
        src = pl.ds(r0 + j * SEG_SKEW, OUT_ROWS)
        y = jnp.concatenate(
            [b_f[c, src, :] + a_f[c, src, :] * cf_ref[c, pl.ds(j, 1), :]
             + b_b[c, src, :] + a_b[c, src, :] * cbk_ref[c, pl.ds(j, 1), :]
             for c in range(n_col)], axis=-1)
        out = y * jax.nn.gelu(gl_ref[rows, :])
        o_ref[rows, :] = _rms(out, gain).astype(BF16)
        return 0

    lax.fori_loop(0, seq // OUT_ROWS, out_chunk, 0)


def _lru(xl, gl, conv_w, conv_b, wg, b_r, b_i, lam, gain, batch, seq):
    t = xl.shape[0]
    row = lambda b: (b, 0)
    const2 = lambda b: (0, 0)
    return pl.pallas_call(
        _lru_kernel,
        grid=(batch,),
        in_specs=[
            pl.BlockSpec((seq, LRU_WIDTH), row),
            pl.BlockSpec((seq, LRU_WIDTH), row),
            pl.BlockSpec((CONV_WIDTH, LRU_WIDTH), const2),
            pl.BlockSpec((1, LRU_WIDTH), const2),
            pl.BlockSpec((2, LRU_WIDTH // 2, 2 * LRU_WIDTH), lambda b: (0, 0, 0)),
            pl.BlockSpec((2, LRU_WIDTH), const2),
            pl.BlockSpec((2, LRU_WIDTH), const2),
            pl.BlockSpec((2, LRU_WIDTH), const2),
            pl.BlockSpec((1, LRU_WIDTH), const2),
        ],
        out_specs=pl.BlockSpec((seq, LRU_WIDTH), row),
        out_shape=jax.ShapeDtypeStruct((t, LRU_WIDTH), BF16),
        scratch_shapes=[
            pltpu.VMEM((LRU_WIDTH // LANES, seq + 2 * SUBLANES, LANES), F32),
            pltpu.VMEM((LRU_WIDTH // LANES, seq + N_SEG * SEG_SKEW, LANES), F32),
            pltpu.VMEM((LRU_WIDTH // LANES, seq + N_SEG * SEG_SKEW, LANES), F32),
            pltpu.VMEM((LRU_WIDTH // LANES, seq + N_SEG * SEG_SKEW, LANES), F32),
            pltpu.VMEM((LRU_WIDTH // LANES, seq + N_SEG * SEG_SKEW, LANES), F32),
            pltpu.VMEM((LRU_WIDTH // LANES, N_SEG, LANES), F32),
            pltpu.VMEM((LRU_WIDTH // LANES, N_SEG, LANES), F32),
        ],
        compiler_params=pltpu.CompilerParams(
            dimension_semantics=("arbitrary",), vmem_limit_bytes=VMEM_LIMIT),
        name="rglru",
    )(xl, gl, conv_w, conv_b, wg, b_r, b_i, lam, gain)


def _ffn_kernel(x_ref, at_ref, lr_ref, wo_ref, g_ref, wg_ref, wu_ref, wd_ref, o_ref, ff_ref):
    h1 = (x_ref[...]
          + jnp.dot(at_ref[...], wo_ref[0:ATTN_WIDTH, :], preferred_element_type=F32)
          + jnp.dot(lr_ref[...], wo_ref[ATTN_WIDTH:, :], preferred_element_type=F32))
    u = _rms(h1, g_ref[...]).astype(BF16)
    o_ref[...] = h1

    for c in range(wg_ref.shape[0]):
        gate = jnp.dot(u, wg_ref[c], preferred_element_type=F32)
        up = jnp.dot(u, wu_ref[c], preferred_element_type=F32)
        ff_ref[:, c * FF_CHUNK:(c + 1) * FF_CHUNK] = (jax.nn.silu(gate) * up).astype(BF16)

    o_ref[...] += jnp.dot(ff_ref[...], wd_ref[...], preferred_element_type=F32)


def _ffn(x2, attn_n, lru_n, w_out, norm_g, wg, wu, wd):
    t = x2.shape[0]
    tm = TM_PROJ
    nc = wg.shape[0]
    row = lambda i: (i, 0)
    const2 = lambda i: (0, 0)
    const3 = lambda i: (0, 0, 0)
    return pl.pallas_call(
        _ffn_kernel,
        grid=(t // tm,),
        in_specs=[
            pl.BlockSpec((tm, D_MODEL), row),
            pl.BlockSpec((tm, ATTN_WIDTH), row),
            pl.BlockSpec((tm, LRU_WIDTH), row),
            pl.BlockSpec((D_MODEL, D_MODEL), const2),
            pl.BlockSpec((1, D_MODEL), const2),
            pl.BlockSpec((nc, D_MODEL, FF_CHUNK), const3),
            pl.BlockSpec((nc, D_MODEL, FF_CHUNK), const3),
            pl.BlockSpec((nc * FF_CHUNK, D_MODEL), const2),
        ],
        out_specs=pl.BlockSpec((tm, D_MODEL), row),
        out_shape=jax.ShapeDtypeStruct((t, D_MODEL), F32),
        scratch_shapes=[pltpu.VMEM((tm, nc * FF_CHUNK), BF16)],
        compiler_params=pltpu.CompilerParams(
            dimension_semantics=("arbitrary",), vmem_limit_bytes=VMEM_LIMIT),
        name="outproj_ffn",
    )(x2, attn_n, lru_n, w_out, norm_g, wg, wu, wd)


def _rope_tables(seq):
    rows = seq // GRID_W
    row = jnp.repeat(jnp.arange(rows, dtype=F32), GRID_W)
    col = jnp.tile(jnp.arange(GRID_W, dtype=F32), rows)
    axis_dim = HEAD_DIM // 2
    inv = ROPE_THETA ** (-jnp.arange(0, axis_dim, 2, dtype=F32) / axis_dim)
    ang_r = row[:, None] * inv[None, :]
    ang_c = col[:, None] * inv[None, :]
    ang = jnp.concatenate([ang_r, ang_r, ang_c, ang_c], axis=-1)
    ang = jnp.concatenate([ang, ang], axis=-1)
    first = (jnp.arange(LANES) % (axis_dim)) < (axis_dim // 2)
    cos = jnp.cos(ang)
    sin = jnp.sin(ang)
    sin_a = jnp.where(first[None, :], -sin, 0.0)
    sin_b = jnp.where(first[None, :], 0.0, sin)
    return cos, sin_a, sin_b


def _block_diag_gate(w, half):
    per = (LRU_WIDTH // 2) // LRU_BLOCK
    out = jnp.zeros((LRU_WIDTH // 2, LRU_WIDTH // 2), w.dtype)
    for j in range(per):
        out = lax.dynamic_update_slice(out, w[half * per + j], (j * LRU_BLOCK, j * LRU_BLOCK))
    return out


def kernel(x, norm_mix, w_in, q_norm, k_norm, conv_w, conv_b, w_rgate, b_rgate, w_igate, b_igate, lru_lambda, out_norm_attn, out_norm_lru, w_out, norm_ffn, w_gate, w_up, w_down):
    batch, seq, _ = x.shape
    depth = norm_mix.shape[0]
    cos2, sin_a, sin_b = _rope_tables(seq)
    head_id = jnp.arange(LANES) // HEAD_DIM
    bd = (head_id[:, None] == head_id[None, :]).astype(BF16)
    bd = jnp.concatenate([bd, bd], axis=0)

    h = x.reshape(batch * seq, D_MODEL)
    for l in range(depth):
        qg = jnp.tile(q_norm[l] * (HEAD_DIM ** -0.5 * LOG2E), 2)[None, :]
        kg = jnp.tile(k_norm[l], 2)[None, :]
        q, k, vt, xl, gl = _inproj(h, norm_mix[l][None, :], w_in[l].astype(BF16),
                                   cos2, sin_a, sin_b, qg, kg, bd, batch, seq)
        attn_n = _attention(q, k, vt, out_norm_attn[l][None, :], batch, seq)

        wg = jnp.stack([
            jnp.concatenate([_block_diag_gate(w_rgate[l, 0], hf), _block_diag_gate(w_igate[l, 0], hf),
                             _block_diag_gate(w_rgate[l, 1], hf), _block_diag_gate(w_igate[l, 1], hf)],
                            axis=1)
            for hf in range(2)])
        wg = (0.5 * wg).astype(BF16)
        lru_n = _lru(xl, gl, conv_w[l], conv_b[l][None, :], wg, b_rgate[l], b_igate[l],
                     lru_lambda[l], out_norm_lru[l][None, :], batch, seq)

        d_ff = w_gate.shape[-1]
        nc = d_ff // FF_CHUNK
        wgc = w_gate[l].astype(BF16).reshape(D_MODEL, nc, FF_CHUNK).transpose(1, 0, 2)
        wuc = w_up[l].astype(BF16).reshape(D_MODEL, nc, FF_CHUNK).transpose(1, 0, 2)
        h = _ffn(h, attn_n, lru_n, w_out[l].astype(BF16), norm_ffn[l][None, :], wgc, wuc,
                 w_down[l].astype(BF16))
    return h.reshape(batch, seq, D_MODEL)
```

```python
import functools

import jax
import jax.numpy as jnp
from jax import lax
from jax.experimental import pallas as pl
from jax.experimental.pallas import tpu as pltpu

F32 = jnp.float32
BF16 = jnp.bfloat16

D_MODEL = 1024
GRID_W = 64
HEAD_DIM = 64
N_Q_HEADS = 8
N_KV_HEADS = 2
ATTN_WIDTH = N_Q_HEADS * HEAD_DIM
KV_WIDTH = N_KV_HEADS * HEAD_DIM
LRU_WIDTH = D_MODEL - ATTN_WIDTH
LRU_BLOCKS = 8
LRU_BLOCK = LRU_WIDTH // LRU_BLOCKS
CONV_WIDTH = 4
LRU_C = 8.0
D_IN = ATTN_WIDTH + 2 * KV_WIDTH + 2 * LRU_WIDTH
ROPE_THETA = 10000.0
EPS = 1e-6

LANES = 128
SUBLANES = 8
MXU_DIM = 256
VMEM_LIMIT = 56 * 1024 * 1024

TM_PROJ = 512
TQ = 512
SHIFT_BOUND_MAX = 40.0
BOUND_MARGIN = 1.02
FF_CHUNK = MXU_DIM
GATE_ROWS = 128
OUT_ROWS = 64
N_SEG = SUBLANES
SEG_SKEW = 1
LOG2E = 1.4426950408889634
V_ROWS = HEAD_DIM + 16


def _rms(x, gain):
    return x * lax.rsqrt(jnp.mean(x * x, axis=-1, keepdims=True) + EPS) * gain


def _inproj_kernel(x_ref, g_ref, w_ref, cos_ref, sa_ref, sb_ref, qg_ref, kg_ref, bd_ref,
                   q_ref, k_ref, vt_ref, xl_ref, gl_ref):
    u = _rms(x_ref[...], g_ref[...]).astype(BF16)
    cos = cos_ref[...]
    sin_a = sa_ref[...]
    sin_b = sb_ref[...]
    bd = bd_ref[...]
    lane = lax.broadcasted_iota(jnp.int32, (1, LANES), 1)
    low = lane < HEAD_DIM

    def norm_rope(z, gain):
        sq = z * z
        hi = sq.astype(BF16)
        lo = (sq - hi.astype(F32)).astype(BF16)
        ss = jnp.dot(jnp.concatenate([hi, lo], axis=-1), bd, preferred_element_type=F32)
        y = z * lax.rsqrt(ss * (1.0 / HEAD_DIM) + EPS) * gain
        return (y * cos + pltpu.roll(y, LANES - 16, 1) * sin_a
                + pltpu.roll(y, 16, 1) * sin_b)

    qg = qg_ref[...]
    zq = jnp.dot(u, w_ref[:, 0:ATTN_WIDTH], preferred_element_type=F32)
    for c in range(ATTN_WIDTH // LANES):
        q_ref[:, c * LANES:(c + 1) * LANES] = norm_rope(
            zq[:, c * LANES:(c + 1) * LANES], qg).astype(BF16)

    o = ATTN_WIDTH
    zkv = jnp.dot(u, w_ref[:, o:o + 2 * KV_WIDTH], preferred_element_type=F32)
    k2 = norm_rope(zkv[:, 0:KV_WIDTH], kg_ref[...])
    v2 = zkv[:, KV_WIDTH:2 * KV_WIDTH]
    o += 2 * KV_WIDTH

    a0 = jnp.where(low, k2, 0.0)
    b1 = jnp.where(low, 0.0, k2)
    k_ref[...] = jnp.concatenate(
        [a0, pltpu.roll(a0, HEAD_DIM, 1), pltpu.roll(b1, HEAD_DIM, 1), b1], axis=-1).astype(BF16)

    vt = v2.T.astype(BF16)
    ones = jnp.ones((V_ROWS - HEAD_DIM, vt.shape[1]), BF16)
    for g in range(N_KV_HEADS):
        vt_ref[0, g, 0:HEAD_DIM, :] = vt[g * HEAD_DIM:(g + 1) * HEAD_DIM, :]
        vt_ref[0, g, HEAD_DIM:V_ROWS, :] = ones
    xl_ref[...] = jnp.dot(u, w_ref[:, o:o + LRU_WIDTH], preferred_element_type=F32)
    o += LRU_WIDTH
    gl_ref[...] = jnp.dot(u, w_ref[:, o:o + LRU_WIDTH], preferred_element_type=F32)


def _inproj(x2, norm_g, w_in, cos2, sin_a, sin_b, qg, kg, bd, batch, seq):
    t = x2.shape[0]
    tm = TM_PROJ
    per_seq = seq // tm
    row = lambda i: (i, 0)
    const = lambda i: (0, 0)
    pos = lambda i: (i % per_seq, 0)
    return pl.pallas_call(
        _inproj_kernel,
        grid=(t // tm,),
        in_specs=[
            pl.BlockSpec((tm, D_MODEL), row),
            pl.BlockSpec((1, D_MODEL), const),
            pl.BlockSpec((D_MODEL, D_IN), const),
            pl.BlockSpec((tm, LANES), pos),
            pl.BlockSpec((tm, LANES), pos),
            pl.BlockSpec((tm, LANES), pos),
            pl.BlockSpec((1, LANES), const),
            pl.BlockSpec((1, LANES), const),
            pl.BlockSpec((2 * LANES, LANES), const),
        ],
        out_specs=[
            pl.BlockSpec((tm, ATTN_WIDTH), row),
            pl.BlockSpec((tm, 4 * LANES), row),
            pl.BlockSpec((1, N_KV_HEADS, V_ROWS, tm), lambda i: (i // per_seq, 0, 0, i % per_seq)),
            pl.BlockSpec((tm, LRU_WIDTH), row),
            pl.BlockSpec((tm, LRU_WIDTH), row),
        ],
        out_shape=[
            jax.ShapeDtypeStruct((t, ATTN_WIDTH), BF16),
            jax.ShapeDtypeStruct((t, 4 * LANES), BF16),
            jax.ShapeDtypeStruct((batch, N_KV_HEADS, V_ROWS, seq), BF16),
            jax.ShapeDtypeStruct((t, LRU_WIDTH), F32),
            jax.ShapeDtypeStruct((t, LRU_WIDTH), F32),
        ],
        compiler_params=pltpu.CompilerParams(
            dimension_semantics=("arbitrary",), vmem_limit_bytes=VMEM_LIMIT),
        name="inproj",
    )(x2, norm_g, w_in, cos2, sin_a, sin_b, qg, kg, bd)


def _attn_kernel(bound_ref, q_ref, k_ref, vt_ref, g_ref, o_ref, ot_ref, s0_ref, s1_ref):
    nt = (((1,), (1,)), ((), ()))
    group = N_Q_HEADS // N_KV_HEADS
    s_slots = (s0_ref, s1_ref)

    def scores(h):
        col = (2 * (h // group) + h % 2) * LANES
        return lax.dot_general(k_ref[:, col:col + LANES], q_ref[:, (h // 2) * LANES:(h // 2 + 1) * LANES],
                               nt, preferred_element_type=F32)

    def weighted_values(h, p):
        ot = jnp.dot(vt_ref[0, h // group], p, preferred_element_type=F32)
        inv = 1.0 / ot[HEAD_DIM:HEAD_DIM + 1, :]
        ot_ref[h * HEAD_DIM:(h + 1) * HEAD_DIM, :] = ot[0:HEAD_DIM, :] * inv

    bound = bound_ref[0, 0]

    @pl.when(bound <= SHIFT_BOUND_MAX)
    def _():
        for h in range(N_Q_HEADS):
            weighted_values(h, jnp.exp2(scores(h) - bound).astype(BF16))

    @pl.when(bound > SHIFT_BOUND_MAX)
    def _():
        def stage_scores(h):
            s = scores(h)
            s_slots[h % 2][...] = s
            return jnp.max(s, axis=0, keepdims=True)

        m = stage_scores(0)
        for h in range(N_Q_HEADS):
            m_next = stage_scores(h + 1) if h + 1 < N_Q_HEADS else None
            weighted_values(h, jnp.exp2(s_slots[h % 2][...] - m).astype(BF16))
            m = m_next

    o_ref[...] = _rms(ot_ref[...].T, g_ref[...]).astype(BF16)


def _attention(bound, q, k, vt, gain, batch, seq):
    t = q.shape[0]
    nq = seq // TQ
    return pl.pallas_call(
        _attn_kernel,
        grid=(batch, nq),
        in_specs=[
            pl.BlockSpec(memory_space=pltpu.SMEM),
            pl.BlockSpec((TQ, ATTN_WIDTH), lambda b, i: (b * nq + i, 0)),
            pl.BlockSpec((seq, 4 * LANES), lambda b, i: (b, 0)),
            pl.BlockSpec((1, N_KV_HEADS, V_ROWS, seq), lambda b, i: (b, 0, 0, 0)),
            pl.BlockSpec((1, ATTN_WIDTH), lambda b, i: (0, 0)),
        ],
        out_specs=pl.BlockSpec((TQ, ATTN_WIDTH), lambda b, i: (b * nq + i, 0)),
        out_shape=jax.ShapeDtypeStruct((t, ATTN_WIDTH), BF16),
        scratch_shapes=[pltpu.VMEM((ATTN_WIDTH, TQ), F32), pltpu.VMEM((seq, TQ), F32),
                        pltpu.VMEM((seq, TQ), F32)],
        compiler_params=pltpu.CompilerParams(
            dimension_semantics=("arbitrary", "arbitrary"), vmem_limit_bytes=VMEM_LIMIT),
        name="attention",
    )(bound, q, k, vt, gain)


def _lru_kernel(xl_ref, gl_ref, cw_ref, cb_ref, wg_ref, br_ref, bi_ref, lam_ref, g_ref,
                o_ref, xcol, a_f, b_f, a_b, b_b, cf_ref, cbk_ref):
    seq = xl_ref.shape[0]
    seg = seq // N_SEG
    sst = seg + SEG_SKEW
    n_col = LRU_WIDTH // LANES
    half_cols = n_col // 2
    half_w = LRU_WIDTH // 2
    pad = SUBLANES

    zeros_pad = jnp.zeros((pad, LANES), F32)
    for c in range(n_col):
        xcol[c, 0:pad, :] = zeros_pad
        xcol[c, pad + seq:pad + seq + pad, :] = zeros_pad
        xcol[c, pad:pad + seq, :] = xl_ref[:, c * LANES:(c + 1) * LANES]

    lam = lam_ref[...]
    k1 = (0.5 * LRU_C) * (jnp.maximum(-lam, 0.0) + jnp.log1p(jnp.exp(-jnp.abs(lam))))
    cw = cw_ref[...]
    cbias = cb_ref[...]
    half_br = 0.5 * br_ref[...]
    half_bi = 0.5 * bi_ref[...]
    a_scr = (a_f, a_b)
    b_scr = (b_f, b_b)

    def gate_chunk(ci, _):
        r0 = pl.multiple_of(ci * GATE_ROWS, GATE_ROWS)
        dst = pl.ds(r0 + (r0 // seg) * SEG_SKEW, GATE_ROWS)
        xc = []
        for c in range(n_col):
            cs = slice(c * LANES, (c + 1) * LANES)
            acc = cbias[:, cs]
            for k in range(CONV_WIDTH):
                acc = acc + xcol[c, pl.ds(r0 + pad - 1 + k, GATE_ROWS), :] * cw[k:k + 1, cs]
            xc.append(acc)
        for h in range(2):
            lhs = jnp.concatenate(xc[h * half_cols:(h + 1) * half_cols], axis=-1).astype(BF16)
            z = jnp.dot(lhs, wg_ref[h], preferred_element_type=F32)
            for cc in range(half_cols):
                c = h * half_cols + cc
                cs = slice(c * LANES, (c + 1) * LANES)
                half_x = 0.5 * xc[c]
                for d in range(2):
                    zo = 2 * d * half_w + cc * LANES
                    zr = z[:, zo:zo + LANES] + half_br[d:d + 1, cs]
                    zi = z[:, zo + half_w:zo + half_w + LANES] + half_bi[d:d + 1, cs]
                    k1d = k1[d:d + 1, cs]
                    w = k1d * jnp.tanh(zr) + k1d
                    a = jnp.exp2(w * (-LOG2E))
                    v = jnp.tanh(w) * (a * a + 1.0)
                    mult = jnp.where(v > 0.0, v * lax.rsqrt(v), 0.0)
                    a_scr[d][c, dst, :] = a
                    b_scr[d][c, dst, :] = mult * ((jnp.tanh(zi) + 1.0) * half_x)
        return 0

    lax.fori_loop(0, seq // GATE_ROWS, gate_chunk, 0)

    def scan_step(t, carry):
        hf, pf, hb, pb = carry
        idx_f = pl.ds(t, N_SEG, stride=sst)
        idx_b = pl.ds(seg - 1 - t, N_SEG, stride=sst)
        nhf, npf, nhb, npb = [], [], [], []
        for c in range(n_col):
            af = a_f[c, idx_f, :]
            h = af * hf[c] + b_f[c, idx_f, :]
            p = af * pf[c]
            b_f[c, idx_f, :] = h
            a_f[c, idx_f, :] = p
            nhf.append(h)
            npf.append(p)
            ab = a_b[c, idx_b, :]
            h = ab * hb[c] + b_b[c, idx_b, :]
            p = ab * pb[c]
            b_b[c, idx_b, :] = h
            a_b[c, idx_b, :] = p
            nhb.append(h)
            npb.append(p)
        return tuple(nhf), tuple(npf), tuple(nhb), tuple(npb)

    zero = tuple(jnp.zeros((N_SEG, LANES), F32) for _ in range(n_col))
    one = tuple(jnp.ones((N_SEG, LANES), F32) for _ in range(n_col))
    hf, pf, hb, pb = lax.fori_loop(0, seg, scan_step, (zero, one, zero, one), unroll=4)

    for c in range(n_col):
        cin = jnp.zeros((1, LANES), F32)
        for j in range(N_SEG):
            cf_ref[c, j:j + 1, :] = cin
            cin = hf[c][j:j + 1, :] + pf[c][j:j + 1, :] * cin
        cin = jnp.zeros((1, LANES), F32)
        for j in range(N_SEG - 1, -1, -1):
            cbk_ref[c, j:j + 1, :] = cin
            cin = hb[c][j:j + 1, :] + pb[c][j:j + 1, :] * cin

    gain = g_ref[...]

    def out_chunk(ci, _):
        r0 = pl.multiple_of(ci * OUT_ROWS, OUT_ROWS)
        rows = pl.ds(r0, OUT_ROWS)
        j = r0 // seg
        src = pl.ds(r0 + j * SEG_SKEW, OUT_ROWS)
        y = jnp.concatenate(
            [b_f[c, src, :] + a_f[c, src, :] * cf_ref[c, pl.ds(j, 1), :]
             + b_b[c, src, :] + a_b[c, src, :] * cbk_ref[c, pl.ds(j, 1), :]
             for c in range(n_col)], axis=-1)
        out = y * jax.nn.gelu(gl_ref[rows, :])
        o_ref[rows, :] = _rms(out, gain).astype(BF16)
        return 0

    lax.fori_loop(0, seq // OUT_ROWS, out_chunk, 0)


def _lru(xl, gl, conv_w, conv_b, wg, b_r, b_i, lam, gain, batch, seq):
    t = xl.shape[0]
    row = lambda b: (b, 0)
    const2 = lambda b: (0, 0)
    return pl.pallas_call(
        _lru_kernel,
        grid=(batch,),
        in_specs=[
            pl.BlockSpec((seq, LRU_WIDTH), row),
            pl.BlockSpec((seq, LRU_WIDTH), row),
            pl.BlockSpec((CONV_WIDTH, LRU_WIDTH), const2),
            pl.BlockSpec((1, LRU_WIDTH), const2),
            pl.BlockSpec((2, LRU_WIDTH // 2, 2 * LRU_WIDTH), lambda b: (0, 0, 0)),
            pl.BlockSpec((2, LRU_WIDTH), const2),
            pl.BlockSpec((2, LRU_WIDTH), const2),
            pl.BlockSpec((2, LRU_WIDTH), const2),
            pl.BlockSpec((1, LRU_WIDTH), const2),
        ],
        out_specs=pl.BlockSpec((seq, LRU_WIDTH), row),
        out_shape=jax.ShapeDtypeStruct((t, LRU_WIDTH), BF16),
        scratch_shapes=[
            pltpu.VMEM((LRU_WIDTH // LANES, seq + 2 * SUBLANES, LANES), F32),
            pltpu.VMEM((LRU_WIDTH // LANES, seq + N_SEG * SEG_SKEW, LANES), F32),
            pltpu.VMEM((LRU_WIDTH // LANES, seq + N_SEG * SEG_SKEW, LANES), F32),
            pltpu.VMEM((LRU_WIDTH // LANES, seq + N_SEG * SEG_SKEW, LANES), F32),
            pltpu.VMEM((LRU_WIDTH // LANES, seq + N_SEG * SEG_SKEW, LANES), F32),
            pltpu.VMEM((LRU_WIDTH // LANES, N_SEG, LANES), F32),
            pltpu.VMEM((LRU_WIDTH // LANES, N_SEG, LANES), F32),
        ],
        compiler_params=pltpu.CompilerParams(
            dimension_semantics=("arbitrary",), vmem_limit_bytes=VMEM_LIMIT),
        name="rglru",
    )(xl, gl, conv_w, conv_b, wg, b_r, b_i, lam, gain)


def _ffn_kernel(x_ref, at_ref, lr_ref, wo_ref, g_ref, wg_ref, wu_ref, wd_ref, o_ref, ff_ref):
    h1 = (x_ref[...]
          + jnp.dot(at_ref[...], wo_ref[0:ATTN_WIDTH, :], preferred_element_type=F32)
          + jnp.dot(lr_ref[...], wo_ref[ATTN_WIDTH:, :], preferred_element_type=F32))
    u = _rms(h1, g_ref[...]).astype(BF16)
    o_ref[...] = h1

    for c in range(wg_ref.shape[0]):
        gate = jnp.dot(u, wg_ref[c], preferred_element_type=F32)
        up = jnp.dot(u, wu_ref[c], preferred_element_type=F32)
        ff_ref[:, c * FF_CHUNK:(c + 1) * FF_CHUNK] = (jax.nn.silu(gate) * up).astype(BF16)

    o_ref[...] += jnp.dot(ff_ref[...], wd_ref[...], preferred_element_type=F32)


def _ffn(x2, attn_n, lru_n, w_out, norm_g, wg, wu, wd):
    t = x2.shape[0]
    tm = TM_PROJ
    nc = wg.shape[0]
    row = lambda i: (i, 0)
    const2 = lambda i: (0, 0)
    const3 = lambda i: (0, 0, 0)
    return pl.pallas_call(
        _ffn_kernel,
        grid=(t // tm,),
        in_specs=[
            pl.BlockSpec((tm, D_MODEL), row),
            pl.BlockSpec((tm, ATTN_WIDTH), row),
            pl.BlockSpec((tm, LRU_WIDTH), row),
            pl.BlockSpec((D_MODEL, D_MODEL), const2),
            pl.BlockSpec((1, D_MODEL), const2),
            pl.BlockSpec((nc, D_MODEL, FF_CHUNK), const3),
            pl.BlockSpec((nc, D_MODEL, FF_CHUNK), const3),
            pl.BlockSpec((nc * FF_CHUNK, D_MODEL), const2),
        ],
        out_specs=pl.BlockSpec((tm, D_MODEL), row),
        out_shape=jax.ShapeDtypeStruct((t, D_MODEL), F32),
        scratch_shapes=[pltpu.VMEM((tm, nc * FF_CHUNK), BF16)],
        compiler_params=pltpu.CompilerParams(
            dimension_semantics=("arbitrary",), vmem_limit_bytes=VMEM_LIMIT),
        name="outproj_ffn",
    )(x2, attn_n, lru_n, w_out, norm_g, wg, wu, wd)


def _rope_tables(seq):
    rows = seq // GRID_W
    row = jnp.repeat(jnp.arange(rows, dtype=F32), GRID_W)
    col = jnp.tile(jnp.arange(GRID_W, dtype=F32), rows)
    axis_dim = HEAD_DIM // 2
    inv = ROPE_THETA ** (-jnp.arange(0, axis_dim, 2, dtype=F32) / axis_dim)
    ang_r = row[:, None] * inv[None, :]
    ang_c = col[:, None] * inv[None, :]
    ang = jnp.concatenate([ang_r, ang_r, ang_c, ang_c], axis=-1)
    ang = jnp.concatenate([ang, ang], axis=-1)
    first = (jnp.arange(LANES) % (axis_dim)) < (axis_dim // 2)
    cos = jnp.cos(ang)
    sin = jnp.sin(ang)
    sin_a = jnp.where(first[None, :], -sin, 0.0)
    sin_b = jnp.where(first[None, :], 0.0, sin)
    return cos, sin_a, sin_b


def _block_diag_gate(w, half):
    per = (LRU_WIDTH // 2) // LRU_BLOCK
    out = jnp.zeros((LRU_WIDTH // 2, LRU_WIDTH // 2), w.dtype)
    for j in range(per):
        out = lax.dynamic_update_slice(out, w[half * per + j], (j * LRU_BLOCK, j * LRU_BLOCK))
    return out


def kernel(x, norm_mix, w_in, q_norm, k_norm, conv_w, conv_b, w_rgate, b_rgate, w_igate, b_igate, lru_lambda, out_norm_attn, out_norm_lru, w_out, norm_ffn, w_gate, w_up, w_down):
    batch, seq, _ = x.shape
    depth = norm_mix.shape[0]
    cos2, sin_a, sin_b = _rope_tables(seq)
    head_id = jnp.arange(LANES) // HEAD_DIM
    bd = (head_id[:, None] == head_id[None, :]).astype(BF16)
    bd = jnp.concatenate([bd, bd], axis=0)

    h = x.reshape(batch * seq, D_MODEL)
    for l in range(depth):
        qg = jnp.tile(q_norm[l] * (HEAD_DIM ** -0.5 * LOG2E), 2)[None, :]
        kg = jnp.tile(k_norm[l], 2)[None, :]
        q, k, vt, xl, gl = _inproj(h, norm_mix[l][None, :], w_in[l].astype(BF16),
                                   cos2, sin_a, sin_b, qg, kg, bd, batch, seq)
        bound = (BOUND_MARGIN * HEAD_DIM * (HEAD_DIM ** -0.5 * LOG2E)
                 * jnp.max(jnp.abs(q_norm[l])) * jnp.max(jnp.abs(k_norm[l]))).reshape(1, 1)
        attn_n = _attention(bound, q, k, vt, out_norm_attn[l][None, :], batch, seq)

        wg = jnp.stack([
            jnp.concatenate([_block_diag_gate(w_rgate[l, 0], hf), _block_diag_gate(w_igate[l, 0], hf),
                             _block_diag_gate(w_rgate[l, 1], hf), _block_diag_gate(w_igate[l, 1], hf)],
                            axis=1)
            for hf in range(2)])
        wg = (0.5 * wg).astype(BF16)
        lru_n = _lru(xl, gl, conv_w[l], conv_b[l][None, :], wg, b_rgate[l], b_igate[l],
                     lru_lambda[l], out_norm_lru[l][None, :], batch, seq)

        d_ff = w_gate.shape[-1]
        nc = d_ff // FF_CHUNK
        wgc = w_gate[l].astype(BF16).reshape(D_MODEL, nc, FF_CHUNK).transpose(1, 0, 2)
        wuc = w_up[l].astype(BF16).reshape(D_MODEL, nc, FF_CHUNK).transpose(1, 0, 2)
        h = _ffn(h, attn_n, lru_n, w_out[l].astype(BF16), norm_ffn[l][None, :], wgc, wuc,
                 w_down[l].astype(BF16))
    return h.reshape(batch, seq, D_MODEL)
```

```python
import functools

import jax
import jax.numpy as jnp
from jax import lax
from jax.experimental import pallas as pl
from jax.experimental.pallas import tpu as pltpu

F32 = jnp.float32
BF16 = jnp.bfloat16

D_MODEL = 1024
GRID_W = 64
HEAD_DIM = 64
N_Q_HEADS = 8
N_KV_HEADS = 2
ATTN_WIDTH = N_Q_HEADS * HEAD_DIM
KV_WIDTH = N_KV_HEADS * HEAD_DIM
LRU_WIDTH = D_MODEL - ATTN_WIDTH
LRU_BLOCKS = 8
LRU_BLOCK = LRU_WIDTH // LRU_BLOCKS
CONV_WIDTH = 4
LRU_C = 8.0
D_IN = ATTN_WIDTH + 2 * KV_WIDTH + 2 * LRU_WIDTH
ROPE_THETA = 10000.0
EPS = 1e-6

LANES = 128
SUBLANES = 8
MXU_DIM = 256
VMEM_LIMIT = 56 * 1024 * 1024

TM_PROJ = 512
TQ = 512
SHIFT_BOUND_MAX = 40.0
BOUND_MARGIN = 1.02
FF_CHUNK = MXU_DIM
GATE_ROWS = 256
OUT_ROWS = 256
N_SEG = SUBLANES
SEG_SKEW = 1
LOG2E = 1.4426950408889634
V_ROWS = HEAD_DIM + 16


def _rms(x, gain):
    return x * lax.rsqrt(jnp.mean(x * x, axis=-1, keepdims=True) + EPS) * gain


def _inproj_kernel(x_ref, g_ref, w_ref, cos_ref, sa_ref, sb_ref, qg_ref, kg_ref, bd_ref,
                   q_ref, k_ref, vt_ref, xl_ref, gl_ref):
    u = _rms(x_ref[...], g_ref[...]).astype(BF16)
    cos = cos_ref[...]
    sin_a = sa_ref[...]
    sin_b = sb_ref[...]
    bd = bd_ref[...]
    lane = lax.broadcasted_iota(jnp.int32, (1, LANES), 1)
    low = lane < HEAD_DIM

    def norm_rope(z, gain):
        sq = z * z
        hi = sq.astype(BF16)
        lo = (sq - hi.astype(F32)).astype(BF16)
        ss = jnp.dot(jnp.concatenate([hi, lo], axis=-1), bd, preferred_element_type=F32)
        y = z * lax.rsqrt(ss * (1.0 / HEAD_DIM) + EPS) * gain
        return (y * cos + pltpu.roll(y, LANES - 16, 1) * sin_a
                + pltpu.roll(y, 16, 1) * sin_b)

    qg = qg_ref[...]
    zq = jnp.dot(u, w_ref[:, 0:ATTN_WIDTH], preferred_element_type=F32)
    for c in range(ATTN_WIDTH // LANES):
        q_ref[:, c * LANES:(c + 1) * LANES] = norm_rope(
            zq[:, c * LANES:(c + 1) * LANES], qg).astype(BF16)

    o = ATTN_WIDTH
    zkv = jnp.dot(u, w_ref[:, o:o + 2 * KV_WIDTH], preferred_element_type=F32)
    k2 = norm_rope(zkv[:, 0:KV_WIDTH], kg_ref[...])
    v2 = zkv[:, KV_WIDTH:2 * KV_WIDTH]
    o += 2 * KV_WIDTH

    a0 = jnp.where(low, k2, 0.0)
    b1 = jnp.where(low, 0.0, k2)
    k_ref[...] = jnp.concatenate(
        [a0, pltpu.roll(a0, HEAD_DIM, 1), pltpu.roll(b1, HEAD_DIM, 1), b1], axis=-1).astype(BF16)

    vt = v2.T.astype(BF16)
    ones = jnp.ones((V_ROWS - HEAD_DIM, vt.shape[1]), BF16)
    for g in range(N_KV_HEADS):
        vt_ref[0, g, 0:HEAD_DIM, :] = vt[g * HEAD_DIM:(g + 1) * HEAD_DIM, :]
        vt_ref[0, g, HEAD_DIM:V_ROWS, :] = ones
    xl_ref[...] = jnp.dot(u, w_ref[:, o:o + LRU_WIDTH], preferred_element_type=F32)
    o += LRU_WIDTH
    gl_ref[...] = jnp.dot(u, w_ref[:, o:o + LRU_WIDTH], preferred_element_type=F32)


def _inproj(x2, norm_g, w_in, cos2, sin_a, sin_b, qg, kg, bd, batch, seq):
    t = x2.shape[0]
    tm = TM_PROJ
    per_seq = seq // tm
    row = lambda i: (i, 0)
    const = lambda i: (0, 0)
    pos = lambda i: (i % per_seq, 0)
    return pl.pallas_call(
        _inproj_kernel,
        grid=(t // tm,),
        in_specs=[
            pl.BlockSpec((tm, D_MODEL), row),
            pl.BlockSpec((1, D_MODEL), const),
            pl.BlockSpec((D_MODEL, D_IN), const),
            pl.BlockSpec((tm, LANES), pos),
            pl.BlockSpec((tm, LANES), pos),
            pl.BlockSpec((tm, LANES), pos),
            pl.BlockSpec((1, LANES), const),
            pl.BlockSpec((1, LANES), const),
            pl.BlockSpec((2 * LANES, LANES), const),
        ],
        out_specs=[
            pl.BlockSpec((tm, ATTN_WIDTH), row),
            pl.BlockSpec((tm, 4 * LANES), row),
            pl.BlockSpec((1, N_KV_HEADS, V_ROWS, tm), lambda i: (i // per_seq, 0, 0, i % per_seq)),
            pl.BlockSpec((tm, LRU_WIDTH), row),
            pl.BlockSpec((tm, LRU_WIDTH), row),
        ],
        out_shape=[
            jax.ShapeDtypeStruct((t, ATTN_WIDTH), BF16),
            jax.ShapeDtypeStruct((t, 4 * LANES), BF16),
            jax.ShapeDtypeStruct((batch, N_KV_HEADS, V_ROWS, seq), BF16),
            jax.ShapeDtypeStruct((t, LRU_WIDTH), F32),
            jax.ShapeDtypeStruct((t, LRU_WIDTH), F32),
        ],
        compiler_params=pltpu.CompilerParams(
            dimension_semantics=("arbitrary",), vmem_limit_bytes=VMEM_LIMIT),
        name="inproj",
    )(x2, norm_g, w_in, cos2, sin_a, sin_b, qg, kg, bd)


def _attn_kernel(bound_ref, q_ref, k_ref, vt_ref, g_ref, o_ref, ot_ref, s0_ref, s1_ref):
    nt = (((1,), (1,)), ((), ()))
    group = N_Q_HEADS // N_KV_HEADS
    s_slots = (s0_ref, s1_ref)

    def scores(h):
        col = (2 * (h // group) + h % 2) * LANES
        return lax.dot_general(k_ref[:, col:col + LANES], q_ref[:, (h // 2) * LANES:(h // 2 + 1) * LANES],
                               nt, preferred_element_type=F32)

    def weighted_values(h, p):
        ot = jnp.dot(vt_ref[0, h // group], p, preferred_element_type=F32)
        inv = 1.0 / ot[HEAD_DIM:HEAD_DIM + 1, :]
        ot_ref[h * HEAD_DIM:(h + 1) * HEAD_DIM, :] = ot[0:HEAD_DIM, :] * inv

    bound = bound_ref[0, 0]

    @pl.when(bound <= SHIFT_BOUND_MAX)
    def _():
        for h in range(N_Q_HEADS):
            weighted_values(h, jnp.exp2(scores(h) - bound).astype(BF16))

    @pl.when(bound > SHIFT_BOUND_MAX)
    def _():
        def stage_scores(h):
            s = scores(h)
            s_slots[h % 2][...] = s
            return jnp.max(s, axis=0, keepdims=True)

        m = stage_scores(0)
        for h in range(N_Q_HEADS):
            m_next = stage_scores(h + 1) if h + 1 < N_Q_HEADS else None
            weighted_values(h, jnp.exp2(s_slots[h % 2][...] - m).astype(BF16))
            m = m_next

    o_ref[...] = _rms(ot_ref[...].T, g_ref[...]).astype(BF16)


def _attention(bound, q, k, vt, gain, batch, seq):
    t = q.shape[0]
    nq = seq // TQ
    return pl.pallas_call(
        _attn_kernel,
        grid=(batch, nq),
        in_specs=[
            pl.BlockSpec(memory_space=pltpu.SMEM),
            pl.BlockSpec((TQ, ATTN_WIDTH), lambda b, i: (b * nq + i, 0)),
            pl.BlockSpec((seq, 4 * LANES), lambda b, i: (b, 0)),
            pl.BlockSpec((1, N_KV_HEADS, V_ROWS, seq), lambda b, i: (b, 0, 0, 0)),
            pl.BlockSpec((1, ATTN_WIDTH), lambda b, i: (0, 0)),
        ],
        out_specs=pl.BlockSpec((TQ, ATTN_WIDTH), lambda b, i: (b * nq + i, 0)),
        out_shape=jax.ShapeDtypeStruct((t, ATTN_WIDTH), BF16),
        scratch_shapes=[pltpu.VMEM((ATTN_WIDTH, TQ), F32), pltpu.VMEM((seq, TQ), F32),
                        pltpu.VMEM((seq, TQ), F32)],
        compiler_params=pltpu.CompilerParams(
            dimension_semantics=("arbitrary", "arbitrary"), vmem_limit_bytes=VMEM_LIMIT),
        name="attention",
    )(bound, q, k, vt, gain)


def _lru_kernel(xl_ref, gl_ref, cw_ref, cb_ref, wg_ref, br_ref, bi_ref, lam_ref, g_ref,
                o_ref, xcol, a_f, b_f, a_b, b_b, cf_ref, cbk_ref):
    seq = xl_ref.shape[0]
    seg = seq // N_SEG
    sst = seg + SEG_SKEW
    n_col = LRU_WIDTH // LANES
    half_cols = n_col // 2
    half_w = LRU_WIDTH // 2
    pad = SUBLANES

    zeros_pad = jnp.zeros((pad, LANES), F32)
    for c in range(n_col):
        xcol[c, 0:pad, :] = zeros_pad
        xcol[c, pad + seq:pad + seq + pad, :] = zeros_pad
        xcol[c, pad:pad + seq, :] = xl_ref[:, c * LANES:(c + 1) * LANES]

    lam = lam_ref[...]
    k1 = (0.5 * LRU_C) * (jnp.maximum(-lam, 0.0) + jnp.log1p(jnp.exp(-jnp.abs(lam))))
    cw = cw_ref[...]
    cbias = cb_ref[...]
    half_br = 0.5 * br_ref[...]
    half_bi = 0.5 * bi_ref[...]
    a_scr = (a_f, a_b)
    b_scr = (b_f, b_b)

    def gate_chunk(ci, _):
        r0 = pl.multiple_of(ci * GATE_ROWS, GATE_ROWS)
        dst = pl.ds(r0 + (r0 // seg) * SEG_SKEW, GATE_ROWS)
        xc = []
        for c in range(n_col):
            cs = slice(c * LANES, (c + 1) * LANES)
            acc = cbias[:, cs]
            for k in range(CONV_WIDTH):
                acc = acc + xcol[c, pl.ds(r0 + pad - 1 + k, GATE_ROWS), :] * cw[k:k + 1, cs]
            xc.append(acc)
        for h in range(2):
            lhs = jnp.concatenate(xc[h * half_cols:(h + 1) * half_cols], axis=-1).astype(BF16)
            z = jnp.dot(lhs, wg_ref[h], preferred_element_type=F32)
            for cc in range(half_cols):
                c = h * half_cols + cc
                cs = slice(c * LANES, (c + 1) * LANES)
                half_x = 0.5 * xc[c]
                for d in range(2):
                    zo = 2 * d * half_w + cc * LANES
                    zr = z[:, zo:zo + LANES] + half_br[d:d + 1, cs]
                    zi = z[:, zo + half_w:zo + half_w + LANES] + half_bi[d:d + 1, cs]
                    k1d = k1[d:d + 1, cs]
                    w = k1d * jnp.tanh(zr) + k1d
                    a = jnp.exp2(w * (-LOG2E))
                    v = jnp.tanh(w) * (a * a + 1.0)
                    mult = jnp.where(v > 0.0, v * lax.rsqrt(v), 0.0)
                    a_scr[d][c, dst, :] = a
                    b_scr[d][c, dst, :] = mult * ((jnp.tanh(zi) + 1.0) * half_x)
        return 0

    lax.fori_loop(0, seq // GATE_ROWS, gate_chunk, 0)

    def scan_step(t, carry):
        hf, pf, hb, pb = carry
        idx_f = pl.ds(t, N_SEG, stride=sst)
        idx_b = pl.ds(seg - 1 - t, N_SEG, stride=sst)
        nhf, npf, nhb, npb = [], [], [], []
        for c in range(n_col):
            af = a_f[c, idx_f, :]
            h = af * hf[c] + b_f[c, idx_f, :]
            p = af * pf[c]
            b_f[c, idx_f, :] = h
            a_f[c, idx_f, :] = p
            nhf.append(h)
            npf.append(p)
            ab = a_b[c, idx_b, :]
            h = ab * hb[c] + b_b[c, idx_b, :]
            p = ab * pb[c]
            b_b[c, idx_b, :] = h
            a_b[c, idx_b, :] = p
            nhb.append(h)
            npb.append(p)
        return tuple(nhf), tuple(npf), tuple(nhb), tuple(npb)

    zero = tuple(jnp.zeros((N_SEG, LANES), F32) for _ in range(n_col))
    one = tuple(jnp.ones((N_SEG, LANES), F32) for _ in range(n_col))
    hf, pf, hb, pb = lax.fori_loop(0, seg, scan_step, (zero, one, zero, one), unroll=4)

    for c in range(n_col):
        cin = jnp.zeros((1, LANES), F32)
        for j in range(N_SEG):
            cf_ref[c, j:j + 1, :] = cin
            cin = hf[c][j:j + 1, :] + pf[c][j:j + 1, :] * cin
        cin = jnp.zeros((1, LANES), F32)
        for j in range(N_SEG - 1, -1, -1):
            cbk_ref[c, j:j + 1, :] = cin
            cin = hb[c][j:j + 1, :] + pb[c][j:j + 1, :] * cin

    gain = g_ref[...]

    def out_chunk(ci, _):
        r0 = pl.multiple_of(ci * OUT_ROWS, OUT_ROWS)
        rows = pl.ds(r0, OUT_ROWS)
        j = r0 // seg
        src = pl.ds(r0 + j * SEG_SKEW, OUT_ROWS)
        y = jnp.concatenate(
            [b_f[c, src, :] + a_f[c, src, :] * cf_ref[c, pl.ds(j, 1), :]
             + b_b[c, src, :] + a_b[c, src, :] * cbk_ref[c, pl.ds(j, 1), :]
             for c in range(n_col)], axis=-1)
        out = y * jax.nn.gelu(gl_ref[rows, :])
        o_ref[rows, :] = _rms(out, gain).astype(BF16)
        return 0

    lax.fori_loop(0, seq // OUT_ROWS, out_chunk, 0)


def _lru(xl, gl, conv_w, conv_b, wg, b_r, b_i, lam, gain, batch, seq):
    t = xl.shape[0]
    row = lambda b: (b, 0)
    const2 = lambda b: (0, 0)
    return pl.pallas_call(
        _lru_kernel,
        grid=(batch,),
        in_specs=[
            pl.BlockSpec((seq, LRU_WIDTH), row),
            pl.BlockSpec((seq, LRU_WIDTH), row),
            pl.BlockSpec((CONV_WIDTH, LRU_WIDTH), const2),
            pl.BlockSpec((1, LRU_WIDTH), const2),
            pl.BlockSpec((2, LRU_WIDTH // 2, 2 * LRU_WIDTH), lambda b: (0, 0, 0)),
            pl.BlockSpec((2, LRU_WIDTH), const2),
            pl.BlockSpec((2, LRU_WIDTH), const2),
            pl.BlockSpec((2, LRU_WIDTH), const2),
            pl.BlockSpec((1, LRU_WIDTH), const2),
        ],
        out_specs=pl.BlockSpec((seq, LRU_WIDTH), row),
        out_shape=jax.ShapeDtypeStruct((t, LRU_WIDTH), BF16),
        scratch_shapes=[
            pltpu.VMEM((LRU_WIDTH // LANES, seq + 2 * SUBLANES, LANES), F32),
            pltpu.VMEM((LRU_WIDTH // LANES, seq + N_SEG * SEG_SKEW, LANES), F32),
            pltpu.VMEM((LRU_WIDTH // LANES, seq + N_SEG * SEG_SKEW, LANES), F32),
            pltpu.VMEM((LRU_WIDTH // LANES, seq + N_SEG * SEG_SKEW, LANES), F32),
            pltpu.VMEM((LRU_WIDTH // LANES, seq + N_SEG * SEG_SKEW, LANES), F32),
            pltpu.VMEM((LRU_WIDTH // LANES, N_SEG, LANES), F32),
            pltpu.VMEM((LRU_WIDTH // LANES, N_SEG, LANES), F32),
        ],
        compiler_params=pltpu.CompilerParams(
            dimension_semantics=("arbitrary",), vmem_limit_bytes=VMEM_LIMIT),
        name="rglru",
    )(xl, gl, conv_w, conv_b, wg, b_r, b_i, lam, gain)


def _ffn_kernel(x_ref, at_ref, lr_ref, wo_ref, g_ref, wg_ref, wu_ref, wd_ref, o_ref, ff_ref):
    h1 = (x_ref[...]
          + jnp.dot(at_ref[...], wo_ref[0:ATTN_WIDTH, :], preferred_element_type=F32)
          + jnp.dot(lr_ref[...], wo_ref[ATTN_WIDTH:, :], preferred_element_type=F32))
    u = _rms(h1, g_ref[...]).astype(BF16)
    o_ref[...] = h1

    for c in range(wg_ref.shape[1] // FF_CHUNK):
        cols = slice(c * FF_CHUNK, (c + 1) * FF_CHUNK)
        gate = jnp.dot(u, wg_ref[:, cols], preferred_element_type=F32)
        up = jnp.dot(u, wu_ref[:, cols], preferred_element_type=F32)
        ff_ref[:, cols] = (jax.nn.silu(gate) * up).astype(BF16)

    o_ref[...] += jnp.dot(ff_ref[...], wd_ref[...], preferred_element_type=F32)


def _ffn(x2, attn_n, lru_n, w_out, norm_g, wg, wu, wd):
    t = x2.shape[0]
    tm = TM_PROJ
    d_ff = wg.shape[1]
    row = lambda i: (i, 0)
    const2 = lambda i: (0, 0)
    return pl.pallas_call(
        _ffn_kernel,
        grid=(t // tm,),
        in_specs=[
            pl.BlockSpec((tm, D_MODEL), row),
            pl.BlockSpec((tm, ATTN_WIDTH), row),
            pl.BlockSpec((tm, LRU_WIDTH), row),
            pl.BlockSpec((D_MODEL, D_MODEL), const2),
            pl.BlockSpec((1, D_MODEL), const2),
            pl.BlockSpec((D_MODEL, d_ff), const2),
            pl.BlockSpec((D_MODEL, d_ff), const2),
            pl.BlockSpec((d_ff, D_MODEL), const2),
        ],
        out_specs=pl.BlockSpec((tm, D_MODEL), row),
        out_shape=jax.ShapeDtypeStruct((t, D_MODEL), F32),
        scratch_shapes=[pltpu.VMEM((tm, d_ff), BF16)],
        compiler_params=pltpu.CompilerParams(
            dimension_semantics=("arbitrary",), vmem_limit_bytes=VMEM_LIMIT),
        name="outproj_ffn",
    )(x2, attn_n, lru_n, w_out, norm_g, wg, wu, wd)


def _rope_tables(seq):
    rows = seq // GRID_W
    row = jnp.repeat(jnp.arange(rows, dtype=F32), GRID_W)
    col = jnp.tile(jnp.arange(GRID_W, dtype=F32), rows)
    axis_dim = HEAD_DIM // 2
    inv = ROPE_THETA ** (-jnp.arange(0, axis_dim, 2, dtype=F32) / axis_dim)
    ang_r = row[:, None] * inv[None, :]
    ang_c = col[:, None] * inv[None, :]
    ang = jnp.concatenate([ang_r, ang_r, ang_c, ang_c], axis=-1)
    ang = jnp.concatenate([ang, ang], axis=-1)
    first = (jnp.arange(LANES) % (axis_dim)) < (axis_dim // 2)
    cos = jnp.cos(ang)
    sin = jnp.sin(ang)
    sin_a = jnp.where(first[None, :], -sin, 0.0)
    sin_b = jnp.where(first[None, :], 0.0, sin)
    return cos, sin_a, sin_b


def _block_diag_gate(w, half):
    per = (LRU_WIDTH // 2) // LRU_BLOCK
    out = jnp.zeros((LRU_WIDTH // 2, LRU_WIDTH // 2), w.dtype)
    for j in range(per):
        out = lax.dynamic_update_slice(out, w[half * per + j], (j * LRU_BLOCK, j * LRU_BLOCK))
    return out


def kernel(x, norm_mix, w_in, q_norm, k_norm, conv_w, conv_b, w_rgate, b_rgate, w_igate, b_igate, lru_lambda, out_norm_attn, out_norm_lru, w_out, norm_ffn, w_gate, w_up, w_down):
    batch, seq, _ = x.shape
    depth = norm_mix.shape[0]
    cos2, sin_a, sin_b = _rope_tables(seq)
    head_id = jnp.arange(LANES) // HEAD_DIM
    bd = (head_id[:, None] == head_id[None, :]).astype(BF16)
    bd = jnp.concatenate([bd, bd], axis=0)

    h = x.reshape(batch * seq, D_MODEL)
    for l in range(depth):
        qg = jnp.tile(q_norm[l] * (HEAD_DIM ** -0.5 * LOG2E), 2)[None, :]
        kg = jnp.tile(k_norm[l], 2)[None, :]
        q, k, vt, xl, gl = _inproj(h, norm_mix[l][None, :], w_in[l].astype(BF16),
                                   cos2, sin_a, sin_b, qg, kg, bd, batch, seq)
        bound = (BOUND_MARGIN * HEAD_DIM * (HEAD_DIM ** -0.5 * LOG2E)
                 * jnp.max(jnp.abs(q_norm[l])) * jnp.max(jnp.abs(k_norm[l]))).reshape(1, 1)
        attn_n = _attention(bound, q, k, vt, out_norm_attn[l][None, :], batch, seq)

        wg = jnp.stack([
            jnp.concatenate([_block_diag_gate(w_rgate[l, 0], hf), _block_diag_gate(w_igate[l, 0], hf),
                             _block_diag_gate(w_rgate[l, 1], hf), _block_diag_gate(w_igate[l, 1], hf)],
                            axis=1)
            for hf in range(2)])
        wg = (0.5 * wg).astype(BF16)
        lru_n = _lru(xl, gl, conv_w[l], conv_b[l][None, :], wg, b_rgate[l], b_igate[l],
                     lru_lambda[l], out_norm_lru[l][None, :], batch, seq)

        h = _ffn(h, attn_n, lru_n, w_out[l].astype(BF16), norm_ffn[l][None, :],
                 w_gate[l].astype(BF16), w_up[l].astype(BF16), w_down[l].astype(BF16))
    return h.reshape(batch, seq, D_MODEL)
```

```python
import functools

import jax
import jax.numpy as jnp
import numpy as np
from jax import lax
from jax.experimental import pallas as pl
from jax.experimental.pallas import tpu as pltpu

F32 = jnp.float32
BF16 = jnp.bfloat16

D_MODEL = 1024
GRID_W = 64
HEAD_DIM = 64
N_Q_HEADS = 8
N_KV_HEADS = 2
ATTN_WIDTH = N_Q_HEADS * HEAD_DIM
KV_WIDTH = N_KV_HEADS * HEAD_DIM
LRU_WIDTH = D_MODEL - ATTN_WIDTH
LRU_BLOCKS = 8
LRU_BLOCK = LRU_WIDTH // LRU_BLOCKS
CONV_WIDTH = 4
LRU_C = 8.0
D_IN = ATTN_WIDTH + 2 * KV_WIDTH + 2 * LRU_WIDTH
ROPE_THETA = 10000.0
EPS = 1e-6

LANES = 128
SUBLANES = 8
MXU_DIM = 256
VMEM_LIMIT = 56 * 1024 * 1024

TM_PROJ = 1024
TM_FFN = 1024
TQ = 1024
SHIFT_BOUND_MAX = 40.0
BOUND_MARGIN = 1.02
FF_CHUNK = MXU_DIM
GATE_ROWS = 256
OUT_ROWS = 256
N_SEG = SUBLANES
SEG_SKEW = 1
LOG2E = 1.4426950408889634
V_ROWS = HEAD_DIM + 16


def _rms(x, gain):
    return x * lax.rsqrt(jnp.mean(x * x, axis=-1, keepdims=True) + EPS) * gain


def _inproj_kernel(x_ref, g_ref, w_ref, cos_ref, sa_ref, sb_ref, qg_ref, kg_ref, bd_ref,
                   q_ref, k_ref, vt_ref, xl_ref, gl_ref):
    u = _rms(x_ref[...], g_ref[...]).astype(BF16)
    cos = cos_ref[...]
    sin_a = sa_ref[...]
    sin_b = sb_ref[...]
    bd = bd_ref[...]
    lane = lax.broadcasted_iota(jnp.int32, (1, LANES), 1)
    low = lane < HEAD_DIM

    def norm_rope(z, gain):
        sq = z * z
        hi = sq.astype(BF16)
        lo = (sq - hi.astype(F32)).astype(BF16)
        ss = jnp.dot(jnp.concatenate([hi, lo], axis=-1), bd, preferred_element_type=F32)
        y = z * lax.rsqrt(ss * (1.0 / HEAD_DIM) + EPS) * gain
        return (y * cos + pltpu.roll(y, LANES - 16, 1) * sin_a
                + pltpu.roll(y, 16, 1) * sin_b)

    qg = qg_ref[...]
    zq = jnp.dot(u, w_ref[:, 0:ATTN_WIDTH], preferred_element_type=F32)
    for c in range(ATTN_WIDTH // LANES):
        q_ref[:, c * LANES:(c + 1) * LANES] = norm_rope(
            zq[:, c * LANES:(c + 1) * LANES], qg).astype(BF16)

    o = ATTN_WIDTH
    zkv = jnp.dot(u, w_ref[:, o:o + 2 * KV_WIDTH], preferred_element_type=F32)
    k2 = norm_rope(zkv[:, 0:KV_WIDTH], kg_ref[...])
    v2 = zkv[:, KV_WIDTH:2 * KV_WIDTH]
    o += 2 * KV_WIDTH

    a0 = jnp.where(low, k2, 0.0)
    b1 = jnp.where(low, 0.0, k2)
    k_ref[...] = jnp.concatenate(
        [a0, pltpu.roll(a0, HEAD_DIM, 1), pltpu.roll(b1, HEAD_DIM, 1), b1], axis=-1).astype(BF16)

    vt = v2.T.astype(BF16)
    ones = jnp.ones((V_ROWS - HEAD_DIM, vt.shape[1]), BF16)
    for g in range(N_KV_HEADS):
        vt_ref[0, g, 0:HEAD_DIM, :] = vt[g * HEAD_DIM:(g + 1) * HEAD_DIM, :]
        vt_ref[0, g, HEAD_DIM:V_ROWS, :] = ones
    xl_ref[...] = jnp.dot(u, w_ref[:, o:o + LRU_WIDTH], preferred_element_type=F32)
    o += LRU_WIDTH
    gl_ref[...] = jnp.dot(u, w_ref[:, o:o + LRU_WIDTH], preferred_element_type=F32)


def _inproj(x2, norm_g, w_in, cos2, sin_a, sin_b, qg, kg, bd, batch, seq):
    t = x2.shape[0]
    tm = TM_PROJ
    per_seq = seq // tm
    row = lambda i: (i, 0)
    const = lambda i: (0, 0)
    pos = lambda i: (i % per_seq, 0)
    return pl.pallas_call(
        _inproj_kernel,
        grid=(t // tm,),
        in_specs=[
            pl.BlockSpec((tm, D_MODEL), row),
            pl.BlockSpec((1, D_MODEL), const),
            pl.BlockSpec((D_MODEL, D_IN), const),
            pl.BlockSpec((tm, LANES), pos),
            pl.BlockSpec((tm, LANES), pos),
            pl.BlockSpec((tm, LANES), pos),
            pl.BlockSpec((1, LANES), const),
            pl.BlockSpec((1, LANES), const),
            pl.BlockSpec((2 * LANES, LANES), const),
        ],
        out_specs=[
            pl.BlockSpec((tm, ATTN_WIDTH), row),
            pl.BlockSpec((tm, 4 * LANES), row),
            pl.BlockSpec((1, N_KV_HEADS, V_ROWS, tm), lambda i: (i // per_seq, 0, 0, i % per_seq)),
            pl.BlockSpec((tm, LRU_WIDTH), row),
            pl.BlockSpec((tm, LRU_WIDTH), row),
        ],
        out_shape=[
            jax.ShapeDtypeStruct((t, ATTN_WIDTH), BF16),
            jax.ShapeDtypeStruct((t, 4 * LANES), BF16),
            jax.ShapeDtypeStruct((batch, N_KV_HEADS, V_ROWS, seq), BF16),
            jax.ShapeDtypeStruct((t, LRU_WIDTH), F32),
            jax.ShapeDtypeStruct((t, LRU_WIDTH), F32),
        ],
        compiler_params=pltpu.CompilerParams(
            dimension_semantics=("arbitrary",), vmem_limit_bytes=VMEM_LIMIT),
        name="inproj",
    )(x2, norm_g, w_in, cos2, sin_a, sin_b, qg, kg, bd)


def _attn_kernel(bound_ref, q_ref, k_ref, vt_ref, g_ref, o_ref, ot_ref, s0_ref, s1_ref):
    nt = (((1,), (1,)), ((), ()))
    group = N_Q_HEADS // N_KV_HEADS
    s_slots = (s0_ref, s1_ref)

    def scores(h):
        col = (2 * (h // group) + h % 2) * LANES
        return lax.dot_general(k_ref[:, col:col + LANES], q_ref[:, (h // 2) * LANES:(h // 2 + 1) * LANES],
                               nt, preferred_element_type=F32)

    def weighted_values(h, p):
        ot = jnp.dot(vt_ref[0, h // group], p, preferred_element_type=F32)
        inv = 1.0 / ot[HEAD_DIM:HEAD_DIM + 1, :]
        ot_ref[h * HEAD_DIM:(h + 1) * HEAD_DIM, :] = ot[0:HEAD_DIM, :] * inv

    bound = bound_ref[0, 0]

    @pl.when(bound <= SHIFT_BOUND_MAX)
    def _():
        for h in range(N_Q_HEADS):
            weighted_values(h, jnp.exp2(scores(h) - bound).astype(BF16))

    @pl.when(bound > SHIFT_BOUND_MAX)
    def _():
        def stage_scores(h):
            s = scores(h)
            s_slots[h % 2][...] = s
            return jnp.max(s, axis=0, keepdims=True)

        m = stage_scores(0)
        for h in range(N_Q_HEADS):
            m_next = stage_scores(h + 1) if h + 1 < N_Q_HEADS else None
            weighted_values(h, jnp.exp2(s_slots[h % 2][...] - m).astype(BF16))
            m = m_next

    o_ref[...] = _rms(ot_ref[...].T, g_ref[...]).astype(BF16)


def _attention(bound, q, k, vt, gain, batch, seq):
    t = q.shape[0]
    nq = seq // TQ
    return pl.pallas_call(
        _attn_kernel,
        grid=(batch, nq),
        in_specs=[
            pl.BlockSpec(memory_space=pltpu.SMEM),
            pl.BlockSpec((TQ, ATTN_WIDTH), lambda b, i: (b * nq + i, 0)),
            pl.BlockSpec((seq, 4 * LANES), lambda b, i: (b, 0)),
            pl.BlockSpec((1, N_KV_HEADS, V_ROWS, seq), lambda b, i: (b, 0, 0, 0)),
            pl.BlockSpec((1, ATTN_WIDTH), lambda b, i: (0, 0)),
        ],
        out_specs=pl.BlockSpec((TQ, ATTN_WIDTH), lambda b, i: (b * nq + i, 0)),
        out_shape=jax.ShapeDtypeStruct((t, ATTN_WIDTH), BF16),
        scratch_shapes=[pltpu.VMEM((ATTN_WIDTH, TQ), F32), pltpu.VMEM((seq, TQ), F32),
                        pltpu.VMEM((seq, TQ), F32)],
        compiler_params=pltpu.CompilerParams(
            dimension_semantics=("arbitrary", "arbitrary"), vmem_limit_bytes=VMEM_LIMIT),
        name="attention",
    )(bound, q, k, vt, gain)


def _lru_kernel(xl_ref, gl_ref, cw_ref, cb_ref, wg_ref, br_ref, bi_ref, lam_ref, g_ref,
                o_ref, xcol, a_f, b_f, a_b, b_b, cf_ref, cbk_ref):
    seq = xl_ref.shape[0]
    seg = seq // N_SEG
    sst = seg + SEG_SKEW
    n_col = LRU_WIDTH // LANES
    half_cols = n_col // 2
    half_w = LRU_WIDTH // 2
    pad = SUBLANES

    zeros_pad = jnp.zeros((pad, LANES), F32)
    for c in range(n_col):
        xcol[c, 0:pad, :] = zeros_pad
        xcol[c, pad + seq:pad + seq + pad, :] = zeros_pad
        xcol[c, pad:pad + seq, :] = xl_ref[:, c * LANES:(c + 1) * LANES]

    lam = lam_ref[...]
    k1 = (0.5 * LRU_C) * (jnp.maximum(-lam, 0.0) + jnp.log1p(jnp.exp(-jnp.abs(lam))))
    cw = cw_ref[...]
    cbias = cb_ref[...]
    half_br = 0.5 * br_ref[...]
    half_bi = 0.5 * bi_ref[...]
    a_scr = (a_f, a_b)
    b_scr = (b_f, b_b)

    def gate_chunk(ci, _):
        r0 = pl.multiple_of(ci * GATE_ROWS, GATE_ROWS)
        dst = pl.ds(r0 + (r0 // seg) * SEG_SKEW, GATE_ROWS)
        xc = []
        for c in range(n_col):
            cs = slice(c * LANES, (c + 1) * LANES)
            acc = cbias[:, cs]
            for k in range(CONV_WIDTH):
                acc = acc + xcol[c, pl.ds(r0 + pad - 1 + k, GATE_ROWS), :] * cw[k:k + 1, cs]
            xc.append(acc)
        for h in range(2):
            lhs = jnp.concatenate(xc[h * half_cols:(h + 1) * half_cols], axis=-1).astype(BF16)
            z = jnp.dot(lhs, wg_ref[h], preferred_element_type=F32)
            for cc in range(half_cols):
                c = h * half_cols + cc
                cs = slice(c * LANES, (c + 1) * LANES)
                half_x = 0.5 * xc[c]
                for d in range(2):
                    zo = 2 * d * half_w + cc * LANES
                    zr = z[:, zo:zo + LANES] + half_br[d:d + 1, cs]
                    zi = z[:, zo + half_w:zo + half_w + LANES] + half_bi[d:d + 1, cs]
                    k1d = k1[d:d + 1, cs]
                    w = k1d * jnp.tanh(zr) + k1d
                    a = jnp.exp2(w * (-LOG2E))
                    v = jnp.tanh(w) * (a * a + 1.0)
                    mult = jnp.where(v > 0.0, v * lax.rsqrt(v), 0.0)
                    a_scr[d][c, dst, :] = a
                    b_scr[d][c, dst, :] = mult * ((jnp.tanh(zi) + 1.0) * half_x)
        return 0

    lax.fori_loop(0, seq // GATE_ROWS, gate_chunk, 0)

    def scan_step(t, carry):
        hf, pf, hb, pb = carry
        idx_f = pl.ds(t, N_SEG, stride=sst)
        idx_b = pl.ds(seg - 1 - t, N_SEG, stride=sst)
        nhf, npf, nhb, npb = [], [], [], []
        for c in range(n_col):
            af = a_f[c, idx_f, :]
            h = af * hf[c] + b_f[c, idx_f, :]
            p = af * pf[c]
            b_f[c, idx_f, :] = h
            a_f[c, idx_f, :] = p
            nhf.append(h)
            npf.append(p)
            ab = a_b[c, idx_b, :]
            h = ab * hb[c] + b_b[c, idx_b, :]
            p = ab * pb[c]
            b_b[c, idx_b, :] = h
            a_b[c, idx_b, :] = p
            nhb.append(h)
            npb.append(p)
        return tuple(nhf), tuple(npf), tuple(nhb), tuple(npb)

    zero = tuple(jnp.zeros((N_SEG, LANES), F32) for _ in range(n_col))
    one = tuple(jnp.ones((N_SEG, LANES), F32) for _ in range(n_col))
    hf, pf, hb, pb = lax.fori_loop(0, seg, scan_step, (zero, one, zero, one), unroll=4)

    for c in range(n_col):
        cin = jnp.zeros((1, LANES), F32)
        for j in range(N_SEG):
            cf_ref[c, j:j + 1, :] = cin
            cin = hf[c][j:j + 1, :] + pf[c][j:j + 1, :] * cin
        cin = jnp.zeros((1, LANES), F32)
        for j in range(N_SEG - 1, -1, -1):
            cbk_ref[c, j:j + 1, :] = cin
            cin = hb[c][j:j + 1, :] + pb[c][j:j + 1, :] * cin

    gain = g_ref[...]

    def out_chunk(ci, _):
        r0 = pl.multiple_of(ci * OUT_ROWS, OUT_ROWS)
        rows = pl.ds(r0, OUT_ROWS)
        j = r0 // seg
        src = pl.ds(r0 + j * SEG_SKEW, OUT_ROWS)
        y = jnp.concatenate(
            [b_f[c, src, :] + a_f[c, src, :] * cf_ref[c, pl.ds(j, 1), :]
             + b_b[c, src, :] + a_b[c, src, :] * cbk_ref[c, pl.ds(j, 1), :]
             for c in range(n_col)], axis=-1)
        out = y * jax.nn.gelu(gl_ref[rows, :])
        o_ref[rows, :] = _rms(out, gain).astype(BF16)
        return 0

    lax.fori_loop(0, seq // OUT_ROWS, out_chunk, 0)


def _lru(xl, gl, conv_w, conv_b, wg, b_r, b_i, lam, gain, batch, seq):
    t = xl.shape[0]
    row = lambda b: (b, 0)
    const2 = lambda b: (0, 0)
    return pl.pallas_call(
        _lru_kernel,
        grid=(batch,),
        in_specs=[
            pl.BlockSpec((seq, LRU_WIDTH), row),
            pl.BlockSpec((seq, LRU_WIDTH), row),
            pl.BlockSpec((CONV_WIDTH, LRU_WIDTH), const2),
            pl.BlockSpec((1, LRU_WIDTH), const2),
            pl.BlockSpec((2, LRU_WIDTH // 2, 2 * LRU_WIDTH), lambda b: (0, 0, 0)),
            pl.BlockSpec((2, LRU_WIDTH), const2),
            pl.BlockSpec((2, LRU_WIDTH), const2),
            pl.BlockSpec((2, LRU_WIDTH), const2),
            pl.BlockSpec((1, LRU_WIDTH), const2),
        ],
        out_specs=pl.BlockSpec((seq, LRU_WIDTH), row),
        out_shape=jax.ShapeDtypeStruct((t, LRU_WIDTH), BF16),
        scratch_shapes=[
            pltpu.VMEM((LRU_WIDTH // LANES, seq + 2 * SUBLANES, LANES), F32),
            pltpu.VMEM((LRU_WIDTH // LANES, seq + N_SEG * SEG_SKEW, LANES), F32),
            pltpu.VMEM((LRU_WIDTH // LANES, seq + N_SEG * SEG_SKEW, LANES), F32),
            pltpu.VMEM((LRU_WIDTH // LANES, seq + N_SEG * SEG_SKEW, LANES), F32),
            pltpu.VMEM((LRU_WIDTH // LANES, seq + N_SEG * SEG_SKEW, LANES), F32),
            pltpu.VMEM((LRU_WIDTH // LANES, N_SEG, LANES), F32),
            pltpu.VMEM((LRU_WIDTH // LANES, N_SEG, LANES), F32),
        ],
        compiler_params=pltpu.CompilerParams(
            dimension_semantics=("arbitrary",), vmem_limit_bytes=VMEM_LIMIT),
        name="rglru",
    )(xl, gl, conv_w, conv_b, wg, b_r, b_i, lam, gain)


def _ffn_kernel(x_ref, at_ref, lr_ref, wo_ref, g_ref, wg_ref, wu_ref, wd_ref, o_ref, ff_ref):
    h1 = (x_ref[...]
          + jnp.dot(at_ref[...], wo_ref[0:ATTN_WIDTH, :], preferred_element_type=F32)
          + jnp.dot(lr_ref[...], wo_ref[ATTN_WIDTH:, :], preferred_element_type=F32))
    u = _rms(h1, g_ref[...]).astype(BF16)
    o_ref[...] = h1

    for c in range(wg_ref.shape[1] // FF_CHUNK):
        cols = slice(c * FF_CHUNK, (c + 1) * FF_CHUNK)
        gate = jnp.dot(u, wg_ref[:, cols], preferred_element_type=F32)
        up = jnp.dot(u, wu_ref[:, cols], preferred_element_type=F32)
        ff_ref[:, cols] = (jax.nn.silu(gate) * up).astype(BF16)

    o_ref[...] += jnp.dot(ff_ref[...], wd_ref[...], preferred_element_type=F32)


def _ffn(x2, attn_n, lru_n, w_out, norm_g, wg, wu, wd):
    t = x2.shape[0]
    tm = TM_FFN
    d_ff = wg.shape[1]
    row = lambda i: (i, 0)
    const2 = lambda i: (0, 0)
    once = pl.Buffered(1)
    return pl.pallas_call(
        _ffn_kernel,
        grid=(t // tm,),
        in_specs=[
            pl.BlockSpec((tm, D_MODEL), row),
            pl.BlockSpec((tm, ATTN_WIDTH), row),
            pl.BlockSpec((tm, LRU_WIDTH), row),
            pl.BlockSpec((D_MODEL, D_MODEL), const2, pipeline_mode=once),
            pl.BlockSpec((1, D_MODEL), const2),
            pl.BlockSpec((D_MODEL, d_ff), const2, pipeline_mode=once),
            pl.BlockSpec((D_MODEL, d_ff), const2, pipeline_mode=once),
            pl.BlockSpec((d_ff, D_MODEL), const2, pipeline_mode=once),
        ],
        out_specs=pl.BlockSpec((tm, D_MODEL), row),
        out_shape=jax.ShapeDtypeStruct((t, D_MODEL), F32),
        scratch_shapes=[pltpu.VMEM((tm, d_ff), BF16)],
        compiler_params=pltpu.CompilerParams(
            dimension_semantics=("arbitrary",), vmem_limit_bytes=VMEM_LIMIT),
        name="outproj_ffn",
    )(x2, attn_n, lru_n, w_out, norm_g, wg, wu, wd)


def _rope_tables(seq):
    f32 = np.float32
    rows = seq // GRID_W
    row = np.repeat(np.arange(rows, dtype=f32), GRID_W)
    col = np.tile(np.arange(GRID_W, dtype=f32), rows)
    axis_dim = HEAD_DIM // 2
    inv = np.power(f32(ROPE_THETA), -np.arange(0, axis_dim, 2, dtype=f32) / f32(axis_dim)).astype(f32)
    ang_r = row[:, None] * inv[None, :]
    ang_c = col[:, None] * inv[None, :]
    ang = np.concatenate([ang_r, ang_r, ang_c, ang_c], axis=-1)
    ang = np.concatenate([ang, ang], axis=-1)
    first = (np.arange(LANES) % axis_dim) < (axis_dim // 2)
    cos = np.cos(ang).astype(f32)
    sin = np.sin(ang).astype(f32)
    sin_a = np.where(first[None, :], -sin, f32(0.0)).astype(f32)
    sin_b = np.where(first[None, :], f32(0.0), sin).astype(f32)
    return jnp.asarray(cos), jnp.asarray(sin_a), jnp.asarray(sin_b)


def _block_diag_halves(w):
    per = (LRU_WIDTH // 2) // LRU_BLOCK
    blocks = w.reshape(2, per, LRU_BLOCK, LRU_BLOCK)
    eye = jnp.eye(per, dtype=w.dtype)
    dense = blocks[:, :, :, None, :] * eye[None, :, None, :, None]
    return dense.reshape(2, LRU_WIDTH // 2, LRU_WIDTH // 2)


def kernel(x, norm_mix, w_in, q_norm, k_norm, conv_w, conv_b, w_rgate, b_rgate, w_igate, b_igate, lru_lambda, out_norm_attn, out_norm_lru, w_out, norm_ffn, w_gate, w_up, w_down):
    batch, seq, _ = x.shape
    depth = norm_mix.shape[0]
    cos2, sin_a, sin_b = _rope_tables(seq)
    head_id = jnp.arange(LANES) // HEAD_DIM
    bd = (head_id[:, None] == head_id[None, :]).astype(BF16)
    bd = jnp.concatenate([bd, bd], axis=0)

    h = x.reshape(batch * seq, D_MODEL)
    for l in range(depth):
        qg = jnp.tile(q_norm[l] * (HEAD_DIM ** -0.5 * LOG2E), 2)[None, :]
        kg = jnp.tile(k_norm[l], 2)[None, :]
        q, k, vt, xl, gl = _inproj(h, norm_mix[l][None, :], w_in[l].astype(BF16),
                                   cos2, sin_a, sin_b, qg, kg, bd, batch, seq)
        bound = (BOUND_MARGIN * HEAD_DIM * (HEAD_DIM ** -0.5 * LOG2E)
                 * jnp.max(jnp.abs(q_norm[l])) * jnp.max(jnp.abs(k_norm[l]))).reshape(1, 1)
        attn_n = _attention(bound, q, k, vt, out_norm_attn[l][None, :], batch, seq)

        wg = jnp.concatenate([_block_diag_halves(w_rgate[l, 0]), _block_diag_halves(w_igate[l, 0]),
                              _block_diag_halves(w_rgate[l, 1]), _block_diag_halves(w_igate[l, 1])],
                             axis=2)
        wg = (0.5 * wg).astype(BF16)
        lru_n = _lru(xl, gl, conv_w[l], conv_b[l][None, :], wg, b_rgate[l], b_igate[l],
                     lru_lambda[l], out_norm_lru[l][None, :], batch, seq)

        h = _ffn(h, attn_n, lru_n, w_out[l].astype(BF16), norm_ffn[l][None, :],
                 w_gate[l].astype(BF16), w_up[l].astype(BF16), w_down[l].astype(BF16))
    return h.reshape(batch, seq, D_MODEL)
```

```python
import functools

import jax
import jax.numpy as jnp
import numpy as np
from jax import lax
from jax.experimental import pallas as pl
from jax.experimental.pallas import tpu as pltpu

F32 = jnp.float32
BF16 = jnp.bfloat16

D_MODEL = 1024
GRID_W = 64
HEAD_DIM = 64
N_Q_HEADS = 8
N_KV_HEADS = 2
ATTN_WIDTH = N_Q_HEADS * HEAD_DIM
KV_WIDTH = N_KV_HEADS * HEAD_DIM
LRU_WIDTH = D_MODEL - ATTN_WIDTH
LRU_BLOCKS = 8
LRU_BLOCK = LRU_WIDTH // LRU_BLOCKS
CONV_WIDTH = 4
LRU_C = 8.0
D_IN = ATTN_WIDTH + 2 * KV_WIDTH + 2 * LRU_WIDTH
ROPE_THETA = 10000.0
EPS = 1e-6

LANES = 128
SUBLANES = 8
MXU_DIM = 256
VMEM_LIMIT = 56 * 1024 * 1024

TM_PROJ = 1024
TM_FFN = 1024
TQ = 512
SHIFT_BOUND_MAX = 40.0
BOUND_MARGIN = 1.02
FF_CHUNK = MXU_DIM
GATE_ROWS = 256
OUT_ROWS = 256
N_SEG = SUBLANES
SEG_SKEW = 1
LOG2E = 1.4426950408889634
V_ROWS = HEAD_DIM + 16


def _params(n_grid_axes):
    return pltpu.CompilerParams(dimension_semantics=("arbitrary",) * n_grid_axes,
                                vmem_limit_bytes=VMEM_LIMIT)


def _rms(x, gain):
    return x * lax.rsqrt(jnp.mean(x * x, axis=-1, keepdims=True) + EPS) * gain


def _inproj_kernel(x_ref, g_ref, w32_ref, cos_ref, sa_ref, sb_ref, qg_ref, kg_ref, bd_ref,
                   wgate32_ref, wup32_ref,
                   q_ref, k_ref, vt_ref, xl_ref, gl_ref, wgate_ref, wup_ref, w_ref):
    @pl.when(pl.program_id(0) == 0)
    def _():
        w_ref[...] = w32_ref[...].astype(BF16)

    wgate_ref[...] = wgate32_ref[...].astype(BF16)
    wup_ref[...] = wup32_ref[...].astype(BF16)

    u = _rms(x_ref[...], g_ref[...]).astype(BF16)
    cos = cos_ref[...]
    sin_a = sa_ref[...]
    sin_b = sb_ref[...]
    bd = bd_ref[...]
    lane = lax.broadcasted_iota(jnp.int32, (1, LANES), 1)
    low = lane < HEAD_DIM

    def norm_rope(z, gain):
        sq = z * z
        hi = sq.astype(BF16)
        lo = (sq - hi.astype(F32)).astype(BF16)
        ss = jnp.dot(jnp.concatenate([hi, lo], axis=-1), bd, preferred_element_type=F32)
        y = z * lax.rsqrt(ss * (1.0 / HEAD_DIM) + EPS) * gain
        return (y * cos + pltpu.roll(y, LANES - 16, 1) * sin_a
                + pltpu.roll(y, 16, 1) * sin_b)

    qg = qg_ref[...]
    zq = jnp.dot(u, w_ref[:, 0:ATTN_WIDTH], preferred_element_type=F32)
    for c in range(ATTN_WIDTH // LANES):
        q_ref[:, c * LANES:(c + 1) * LANES] = norm_rope(
            zq[:, c * LANES:(c + 1) * LANES], qg).astype(BF16)

    o = ATTN_WIDTH
    zkv = jnp.dot(u, w_ref[:, o:o + 2 * KV_WIDTH], preferred_element_type=F32)
    k2 = norm_rope(zkv[:, 0:KV_WIDTH], kg_ref[...])
    v2 = zkv[:, KV_WIDTH:2 * KV_WIDTH]
    o += 2 * KV_WIDTH

    a0 = jnp.where(low, k2, 0.0)
    b1 = jnp.where(low, 0.0, k2)
    k_ref[...] = jnp.concatenate(
        [a0, pltpu.roll(a0, HEAD_DIM, 1), pltpu.roll(b1, HEAD_DIM, 1), b1], axis=-1).astype(BF16)

    vt = v2.T.astype(BF16)
    ones = jnp.ones((V_ROWS - HEAD_DIM, vt.shape[1]), BF16)
    for g in range(N_KV_HEADS):
        vt_ref[0, g, 0:HEAD_DIM, :] = vt[g * HEAD_DIM:(g + 1) * HEAD_DIM, :]
        vt_ref[0, g, HEAD_DIM:V_ROWS, :] = ones
    xl_ref[...] = jnp.dot(u, w_ref[:, o:o + LRU_WIDTH], preferred_element_type=F32)
    o += LRU_WIDTH
    gl_ref[...] = jnp.dot(u, w_ref[:, o:o + LRU_WIDTH], preferred_element_type=F32)


def _inproj(x2, norm_g, w_in, cos2, sin_a, sin_b, qg, kg, bd, w_gate, w_up, batch, seq):
    t = x2.shape[0]
    tm = TM_PROJ
    steps = t // tm
    per_seq = seq // tm
    d_ff = w_gate.shape[1]
    slab = D_MODEL // steps
    row = lambda i: (i, 0)
    const = lambda i: (0, 0)
    pos = lambda i: (i % per_seq, 0)
    return pl.pallas_call(
        _inproj_kernel,
        grid=(steps,),
        in_specs=[
            pl.BlockSpec((tm, D_MODEL), row),
            pl.BlockSpec((1, D_MODEL), const),
            pl.BlockSpec((D_MODEL, D_IN), const, pipeline_mode=pl.Buffered(1)),
            pl.BlockSpec((tm, LANES), pos),
            pl.BlockSpec((tm, LANES), pos),
            pl.BlockSpec((tm, LANES), pos),
            pl.BlockSpec((1, LANES), const),
            pl.BlockSpec((1, LANES), const),
            pl.BlockSpec((2 * LANES, LANES), const),
            pl.BlockSpec((slab, d_ff), row),
            pl.BlockSpec((slab, d_ff), row),
        ],
        out_specs=[
            pl.BlockSpec((tm, ATTN_WIDTH), row),
            pl.BlockSpec((tm, 4 * LANES), row),
            pl.BlockSpec((1, N_KV_HEADS, V_ROWS, tm), lambda i: (i // per_seq, 0, 0, i % per_seq)),
            pl.BlockSpec((tm, LRU_WIDTH), row),
            pl.BlockSpec((tm, LRU_WIDTH), row),
            pl.BlockSpec((slab, d_ff), row),
            pl.BlockSpec((slab, d_ff), row),
        ],
        out_shape=[
            jax.ShapeDtypeStruct((t, ATTN_WIDTH), BF16),
            jax.ShapeDtypeStruct((t, 4 * LANES), BF16),
            jax.ShapeDtypeStruct((batch, N_KV_HEADS, V_ROWS, seq), BF16),
            jax.ShapeDtypeStruct((t, LRU_WIDTH), F32),
            jax.ShapeDtypeStruct((t, LRU_WIDTH), F32),
            jax.ShapeDtypeStruct((D_MODEL, d_ff), BF16),
            jax.ShapeDtypeStruct((D_MODEL, d_ff), BF16),
        ],
        scratch_shapes=[pltpu.VMEM((D_MODEL, D_IN), BF16)],
        compiler_params=_params(1),
        name="inproj",
    )(x2, norm_g, w_in, cos2, sin_a, sin_b, qg, kg, bd, w_gate, w_up)


def _attn_kernel(bound_ref, q_ref, k_ref, vt_ref, g_ref, o_ref, ot_ref, s0_ref, s1_ref):
    nt = (((1,), (1,)), ((), ()))
    group = N_Q_HEADS // N_KV_HEADS
    s_slots = (s0_ref, s1_ref)

    def scores(h):
        col = (2 * (h // group) + h % 2) * LANES
        return lax.dot_general(k_ref[:, col:col + LANES], q_ref[:, (h // 2) * LANES:(h // 2 + 1) * LANES],
                               nt, preferred_element_type=F32)

    def weighted_values(h, p):
        ot = jnp.dot(vt_ref[0, h // group], p, preferred_element_type=F32)
        inv = 1.0 / ot[HEAD_DIM:HEAD_DIM + 1, :]
        ot_ref[h * HEAD_DIM:(h + 1) * HEAD_DIM, :] = ot[0:HEAD_DIM, :] * inv

    bound = bound_ref[0, 0]

    @pl.when(bound <= SHIFT_BOUND_MAX)
    def _():
        for h in range(N_Q_HEADS):
            weighted_values(h, jnp.exp2(scores(h) - bound).astype(BF16))

    @pl.when(bound > SHIFT_BOUND_MAX)
    def _():
        def stage_scores(h):
            s = scores(h)
            s_slots[h % 2][...] = s
            return jnp.max(s, axis=0, keepdims=True)

        m = stage_scores(0)
        for h in range(N_Q_HEADS):
            m_next = stage_scores(h + 1) if h + 1 < N_Q_HEADS else None
            weighted_values(h, jnp.exp2(s_slots[h % 2][...] - m).astype(BF16))
            m = m_next

    ot = ot_ref[...]
    scale = lax.rsqrt(jnp.mean(ot * ot, axis=0, keepdims=True) + EPS)
    for c in range(ot.shape[1] // LANES):
        cols = slice(c * LANES, (c + 1) * LANES)
        o_ref[0, :, cols] = (ot[:, cols] * scale[:, cols] * g_ref[...]).astype(BF16)


def _attention(bound, q, k, vt, gain, batch, seq):
    t = q.shape[0]
    nq = seq // TQ
    return pl.pallas_call(
        _attn_kernel,
        grid=(batch, nq),
        in_specs=[
            pl.BlockSpec(memory_space=pltpu.SMEM),
            pl.BlockSpec((TQ, ATTN_WIDTH), lambda b, i: (b * nq + i, 0)),
            pl.BlockSpec((seq, 4 * LANES), lambda b, i: (b, 0)),
            pl.BlockSpec((1, N_KV_HEADS, V_ROWS, seq), lambda b, i: (b, 0, 0, 0)),
            pl.BlockSpec((ATTN_WIDTH, LANES), lambda b, i: (0, 0)),
        ],
        out_specs=pl.BlockSpec((1, ATTN_WIDTH, TQ), lambda b, i: (b, 0, i)),
        out_shape=jax.ShapeDtypeStruct((batch, ATTN_WIDTH, seq), BF16),
        scratch_shapes=[pltpu.VMEM((ATTN_WIDTH, TQ), F32), pltpu.VMEM((seq, TQ), F32),
                        pltpu.VMEM((seq, TQ), F32)],
        compiler_params=_params(2),
        name="attention",
    )(bound, q, k, vt, gain)


def _lru_kernel(xl_ref, gl_ref, cw_ref, cb_ref, wg_ref, br_ref, bi_ref, lam_ref, g_ref,
                wdown32_ref, wout32_ref, o_ref, wdown_ref, wout_ref,
                xcol, a_f, b_f, a_b, b_b, cf_ref, cbk_ref):
    wdown_ref[...] = wdown32_ref[...].astype(BF16)
    wout_ref[...] = wout32_ref[...].astype(BF16)

    seq = xl_ref.shape[0]
    seg = seq // N_SEG
    sst = seg + SEG_SKEW
    n_col = LRU_WIDTH // LANES
    half_cols = n_col // 2
    half_w = LRU_WIDTH // 2
    pad = SUBLANES

    zeros_pad = jnp.zeros((pad, LANES), F32)
    for c in range(n_col):
        xcol[c, 0:pad, :] = zeros_pad
        xcol[c, pad + seq:pad + seq + pad, :] = zeros_pad
        xcol[c, pad:pad + seq, :] = xl_ref[:, c * LANES:(c + 1) * LANES]

    lam = lam_ref[...]
    k1 = (0.5 * LRU_C) * (jnp.maximum(-lam, 0.0) + jnp.log1p(jnp.exp(-jnp.abs(lam))))
    cw = cw_ref[...]
    cbias = cb_ref[...]
    half_br = 0.5 * br_ref[...]
    half_bi = 0.5 * bi_ref[...]
    a_scr = (a_f, a_b)
    b_scr = (b_f, b_b)

    def gate_chunk(ci, _):
        r0 = pl.multiple_of(ci * GATE_ROWS, GATE_ROWS)
        dst = pl.ds(r0 + (r0 // seg) * SEG_SKEW, GATE_ROWS)
        xc = []
        for c in range(n_col):
            cs = slice(c * LANES, (c + 1) * LANES)
            acc = cbias[:, cs]
            for k in range(CONV_WIDTH):
                acc = acc + xcol[c, pl.ds(r0 + pad - 1 + k, GATE_ROWS), :] * cw[k:k + 1, cs]
            xc.append(acc)
        for h in range(2):
            lhs = jnp.concatenate(xc[h * half_cols:(h + 1) * half_cols], axis=-1).astype(BF16)
            z = jnp.dot(lhs, wg_ref[h], preferred_element_type=F32)
            for cc in range(half_cols):
                c = h * half_cols + cc
                cs = slice(c * LANES, (c + 1) * LANES)
                half_x = 0.5 * xc[c]
                for d in range(2):
                    zo = 2 * d * half_w + cc * LANES
                    zr = z[:, zo:zo + LANES] + half_br[d:d + 1, cs]
                    zi = z[:, zo + half_w:zo + half_w + LANES] + half_bi[d:d + 1, cs]
                    k1d = k1[d:d + 1, cs]
                    w = k1d * jnp.tanh(zr) + k1d
                    a = jnp.exp2(w * (-LOG2E))
                    v = jnp.tanh(w) * (a * a + 1.0)
                    mult = jnp.where(v > 0.0, v * lax.rsqrt(v), 0.0)
                    a_scr[d][c, dst, :] = a
                    b_scr[d][c, dst, :] = mult * ((jnp.tanh(zi) + 1.0) * half_x)
        return 0

    lax.fori_loop(0, seq // GATE_ROWS, gate_chunk, 0)

    def scan_step(t, carry):
        hf, pf, hb, pb = carry
        idx_f = pl.ds(t, N_SEG, stride=sst)
        idx_b = pl.ds(seg - 1 - t, N_SEG, stride=sst)
        nhf, npf, nhb, npb = [], [], [], []
        for c in range(n_col):
            af = a_f[c, idx_f, :]
            h = af * hf[c] + b_f[c, idx_f, :]
            p = af * pf[c]
            b_f[c, idx_f, :] = h
            a_f[c, idx_f, :] = p
            nhf.append(h)
            npf.append(p)
            ab = a_b[c, idx_b, :]
            h = ab * hb[c] + b_b[c, idx_b, :]
            p = ab * pb[c]
            b_b[c, idx_b, :] = h
            a_b[c, idx_b, :] = p
            nhb.append(h)
            npb.append(p)
        return tuple(nhf), tuple(npf), tuple(nhb), tuple(npb)

    zero = tuple(jnp.zeros((N_SEG, LANES), F32) for _ in range(n_col))
    one = tuple(jnp.ones((N_SEG, LANES), F32) for _ in range(n_col))
    hf, pf, hb, pb = lax.fori_loop(0, seg, scan_step, (zero, one, zero, one), unroll=4)

    for c in range(n_col):
        cin = jnp.zeros((1, LANES), F32)
        for j in range(N_SEG):
            cf_ref[c, j:j + 1, :] = cin
            cin = hf[c][j:j + 1, :] + pf[c][j:j + 1, :] * cin
        cin = jnp.zeros((1, LANES), F32)
        for j in range(N_SEG - 1, -1, -1):
            cbk_ref[c, j:j + 1, :] = cin
            cin = hb[c][j:j + 1, :] + pb[c][j:j + 1, :] * cin

    gain = g_ref[...]

    def out_chunk(ci, _):
        r0 = pl.multiple_of(ci * OUT_ROWS, OUT_ROWS)
        rows = pl.ds(r0, OUT_ROWS)
        j = r0 // seg
        src = pl.ds(r0 + j * SEG_SKEW, OUT_ROWS)
        y = jnp.concatenate(
            [b_f[c, src, :] + a_f[c, src, :] * cf_ref[c, pl.ds(j, 1), :]
             + b_b[c, src, :] + a_b[c, src, :] * cbk_ref[c, pl.ds(j, 1), :]
             for c in range(n_col)], axis=-1)
        out = y * jax.nn.gelu(gl_ref[rows, :])
        o_ref[rows, :] = _rms(out, gain).astype(BF16)
        return 0

    lax.fori_loop(0, seq // OUT_ROWS, out_chunk, 0)


def _lru(xl, gl, conv_w, conv_b, wg, b_r, b_i, lam, gain, w_down, w_out, batch, seq):
    t = xl.shape[0]
    d_ff = w_down.shape[0]
    row = lambda b: (b, 0)
    const2 = lambda b: (0, 0)
    return pl.pallas_call(
        _lru_kernel,
        grid=(batch,),
        in_specs=[
            pl.BlockSpec((seq, LRU_WIDTH), row),
            pl.BlockSpec((seq, LRU_WIDTH), row),
            pl.BlockSpec((CONV_WIDTH, LRU_WIDTH), const2),
            pl.BlockSpec((1, LRU_WIDTH), const2),
            pl.BlockSpec((2, LRU_WIDTH // 2, 2 * LRU_WIDTH), lambda b: (0, 0, 0)),
            pl.BlockSpec((2, LRU_WIDTH), const2),
            pl.BlockSpec((2, LRU_WIDTH), const2),
            pl.BlockSpec((2, LRU_WIDTH), const2),
            pl.BlockSpec((1, LRU_WIDTH), const2),
            pl.BlockSpec((d_ff // batch, D_MODEL), row),
            pl.BlockSpec((D_MODEL // batch, D_MODEL), row),
        ],
        out_specs=[
            pl.BlockSpec((seq, LRU_WIDTH), row),
            pl.BlockSpec((d_ff // batch, D_MODEL), row),
            pl.BlockSpec((D_MODEL // batch, D_MODEL), row),
        ],
        out_shape=[
            jax.ShapeDtypeStruct((t, LRU_WIDTH), BF16),
            jax.ShapeDtypeStruct((d_ff, D_MODEL), BF16),
            jax.ShapeDtypeStruct((D_MODEL, D_MODEL), BF16),
        ],
        scratch_shapes=[
            pltpu.VMEM((LRU_WIDTH // LANES, seq + 2 * SUBLANES, LANES), F32),
            pltpu.VMEM((LRU_WIDTH // LANES, seq + N_SEG * SEG_SKEW, LANES), F32),
            pltpu.VMEM((LRU_WIDTH // LANES, seq + N_SEG * SEG_SKEW, LANES), F32),
            pltpu.VMEM((LRU_WIDTH // LANES, seq + N_SEG * SEG_SKEW, LANES), F32),
            pltpu.VMEM((LRU_WIDTH // LANES, seq + N_SEG * SEG_SKEW, LANES), F32),
            pltpu.VMEM((LRU_WIDTH // LANES, N_SEG, LANES), F32),
            pltpu.VMEM((LRU_WIDTH // LANES, N_SEG, LANES), F32),
        ],
        compiler_params=_params(1),
        name="rglru",
    )(xl, gl, conv_w, conv_b, wg, b_r, b_i, lam, gain, w_down, w_out)


def _ffn_kernel(x_ref, at_ref, lr_ref, wo_ref, g_ref, wg_ref, wu_ref, wd_ref, o_ref, ff_ref):
    tn = (((0,), (0,)), ((), ()))
    h1 = (x_ref[...]
          + lax.dot_general(at_ref[0], wo_ref[0:ATTN_WIDTH, :], tn, preferred_element_type=F32)
          + jnp.dot(lr_ref[...], wo_ref[ATTN_WIDTH:, :], preferred_element_type=F32))
    u = _rms(h1, g_ref[...]).astype(BF16)
    o_ref[...] = h1

    for c in range(wg_ref.shape[1] // FF_CHUNK):
        cols = slice(c * FF_CHUNK, (c + 1) * FF_CHUNK)
        gate = jnp.dot(u, wg_ref[:, cols], preferred_element_type=F32)
        up = jnp.dot(u, wu_ref[:, cols], preferred_element_type=F32)
        ff_ref[:, cols] = (jax.nn.silu(gate) * up).astype(BF16)

    o_ref[...] += jnp.dot(ff_ref[...], wd_ref[...], preferred_element_type=F32)


def _ffn(x2, attn_nt, lru_n, w_out, norm_g, wg, wu, wd):
    t = x2.shape[0]
    tm = TM_FFN
    d_ff = wg.shape[1]
    per_seq = attn_nt.shape[2] // tm
    row = lambda i: (i, 0)
    const2 = lambda i: (0, 0)
    once = pl.Buffered(1)
    return pl.pallas_call(
        _ffn_kernel,
        grid=(t // tm,),
        in_specs=[
            pl.BlockSpec((tm, D_MODEL), row),
            pl.BlockSpec((1, ATTN_WIDTH, tm), lambda i: (i // per_seq, 0, i % per_seq)),
            pl.BlockSpec((tm, LRU_WIDTH), row),
            pl.BlockSpec((D_MODEL, D_MODEL), const2, pipeline_mode=once),
            pl.BlockSpec((1, D_MODEL), const2),
            pl.BlockSpec((D_MODEL, d_ff), const2, pipeline_mode=once),
            pl.BlockSpec((D_MODEL, d_ff), const2, pipeline_mode=once),
            pl.BlockSpec((d_ff, D_MODEL), const2, pipeline_mode=once),
        ],
        out_specs=pl.BlockSpec((tm, D_MODEL), row),
        out_shape=jax.ShapeDtypeStruct((t, D_MODEL), F32),
        scratch_shapes=[pltpu.VMEM((tm, d_ff), BF16)],
        compiler_params=_params(1),
        name="outproj_ffn",
    )(x2, attn_nt, lru_n, w_out, norm_g, wg, wu, wd)


def _rope_tables(seq):
    f32 = np.float32
    rows = seq // GRID_W
    row = np.repeat(np.arange(rows, dtype=f32), GRID_W)
    col = np.tile(np.arange(GRID_W, dtype=f32), rows)
    axis_dim = HEAD_DIM // 2
    inv = np.power(f32(ROPE_THETA), -np.arange(0, axis_dim, 2, dtype=f32) / f32(axis_dim)).astype(f32)
    ang_r = row[:, None] * inv[None, :]
    ang_c = col[:, None] * inv[None, :]
    ang = np.concatenate([ang_r, ang_r, ang_c, ang_c], axis=-1)
    ang = np.concatenate([ang, ang], axis=-1)
    first = (np.arange(LANES) % axis_dim) < (axis_dim // 2)
    cos = np.cos(ang).astype(f32)
    sin = np.sin(ang).astype(f32)
    sin_a = np.where(first[None, :], -sin, f32(0.0)).astype(f32)
    sin_b = np.where(first[None, :], f32(0.0), sin).astype(f32)
    return jnp.asarray(cos), jnp.asarray(sin_a), jnp.asarray(sin_b)


def _block_diag_halves(w):
    per = (LRU_WIDTH // 2) // LRU_BLOCK
    blocks = w.reshape(2, per, LRU_BLOCK, LRU_BLOCK)
    eye = jnp.eye(per, dtype=w.dtype)
    dense = blocks[:, :, :, None, :] * eye[None, :, None, :, None]
    return dense.reshape(2, LRU_WIDTH // 2, LRU_WIDTH // 2)


def kernel(x, norm_mix, w_in, q_norm, k_norm, conv_w, conv_b, w_rgate, b_rgate, w_igate, b_igate, lru_lambda, out_norm_attn, out_norm_lru, w_out, norm_ffn, w_gate, w_up, w_down):
    batch, seq, _ = x.shape
    depth = norm_mix.shape[0]
    cos2, sin_a, sin_b = _rope_tables(seq)
    head_id = jnp.arange(LANES) // HEAD_DIM
    bd = (head_id[:, None] == head_id[None, :]).astype(BF16)
    bd = jnp.concatenate([bd, bd], axis=0)

    h = x.reshape(batch * seq, D_MODEL)
    for l in range(depth):
        qg = jnp.tile(q_norm[l] * (HEAD_DIM ** -0.5 * LOG2E), 2)[None, :]
        kg = jnp.tile(k_norm[l], 2)[None, :]
        q, k, vt, xl, gl, w_gate16, w_up16 = _inproj(
            h, norm_mix[l][None, :], w_in[l], cos2, sin_a, sin_b, qg, kg, bd,
            w_gate[l], w_up[l], batch, seq)
        bound = (BOUND_MARGIN * HEAD_DIM * (HEAD_DIM ** -0.5 * LOG2E)
                 * jnp.max(jnp.abs(q_norm[l])) * jnp.max(jnp.abs(k_norm[l]))).reshape(1, 1)
        attn_gain = jnp.broadcast_to(out_norm_attn[l][:, None], (ATTN_WIDTH, LANES))
        attn_nt = _attention(bound, q, k, vt, attn_gain, batch, seq)

        wg = jnp.concatenate([_block_diag_halves(w_rgate[l, 0]), _block_diag_halves(w_igate[l, 0]),
                              _block_diag_halves(w_rgate[l, 1]), _block_diag_halves(w_igate[l, 1])],
                             axis=2)
        wg = (0.5 * wg).astype(BF16)
        lru_n, w_down16, w_out16 = _lru(
            xl, gl, conv_w[l], conv_b[l][None, :], wg, b_rgate[l], b_igate[l],
            lru_lambda[l], out_norm_lru[l][None, :], w_down[l], w_out[l], batch, seq)

        h = _ffn(h, attn_nt, lru_n, w_out16, norm_ffn[l][None, :], w_gate16, w_up16, w_down16)
    return h.reshape(batch, seq, D_MODEL)
```

```python
import jax
import jax.numpy as jnp
import numpy as np
from jax import lax
from jax.experimental import pallas as pl
from jax.experimental.pallas import tpu as pltpu

F32 = jnp.float32
BF16 = jnp.bfloat16

D_MODEL = 1024
GRID_W = 64
HEAD_DIM = 64
N_Q_HEADS = 8
N_KV_HEADS = 2
ATTN_WIDTH = N_Q_HEADS * HEAD_DIM
KV_WIDTH = N_KV_HEADS * HEAD_DIM
LRU_WIDTH = D_MODEL - ATTN_WIDTH
LRU_BLOCKS = 8
LRU_BLOCK = LRU_WIDTH // LRU_BLOCKS
CONV_WIDTH = 4
LRU_C = 8.0
D_IN = ATTN_WIDTH + 2 * KV_WIDTH + 2 * LRU_WIDTH
ROPE_THETA = 10000.0
EPS = 1e-6

LANES = 128
SUBLANES = 8
MXU_DIM = 256
VMEM_LIMIT = 56 * 1024 * 1024

TM_PROJ = 1024
TM_FFN = 1024
TQ = 512
SHIFT_BOUND_MAX = 40.0
BOUND_MARGIN = 1.02
FF_CHUNK = MXU_DIM
GATE_ROWS = 256
OUT_ROWS = 256
N_SEG = SUBLANES
SEG_SKEW = 1
LOG2E = 1.4426950408889634
V_ROWS = HEAD_DIM + 16


def _params(n_grid_axes):
    return pltpu.CompilerParams(dimension_semantics=("arbitrary",) * n_grid_axes,
                                vmem_limit_bytes=VMEM_LIMIT)


def _rms(x, gain):
    return x * lax.rsqrt(jnp.mean(x * x, axis=-1, keepdims=True) + EPS) * gain


def _inproj_kernel(x_ref, g_ref, w32_ref, cos_ref, sa_ref, sb_ref, qg_ref, kg_ref, bd_ref,
                   wgate32_ref, wup32_ref,
                   q_ref, k_ref, vt_ref, xl_ref, gl_ref, wgate_ref, wup_ref, w_ref):
    @pl.when(pl.program_id(0) == 0)
    def _():
        w_ref[...] = w32_ref[...].astype(BF16)

    wgate_ref[...] = wgate32_ref[...].astype(BF16)
    wup_ref[...] = wup32_ref[...].astype(BF16)

    u = _rms(x_ref[...], g_ref[...]).astype(BF16)
    cos = cos_ref[...]
    sin_a = sa_ref[...]
    sin_b = sb_ref[...]
    bd = bd_ref[...]
    lane = lax.broadcasted_iota(jnp.int32, (1, LANES), 1)
    low = lane < HEAD_DIM

    def norm_rope(z, gain):
        sq = z * z
        hi = sq.astype(BF16)
        lo = (sq - hi.astype(F32)).astype(BF16)
        ss = jnp.dot(jnp.concatenate([hi, lo], axis=-1), bd, preferred_element_type=F32)
        y = z * lax.rsqrt(ss * (1.0 / HEAD_DIM) + EPS) * gain
        return (y * cos + pltpu.roll(y, LANES - 16, 1) * sin_a
                + pltpu.roll(y, 16, 1) * sin_b)

    qg = qg_ref[...]
    zq = jnp.dot(u, w_ref[:, 0:ATTN_WIDTH], preferred_element_type=F32)
    for c in range(ATTN_WIDTH // LANES):
        q_ref[:, c * LANES:(c + 1) * LANES] = norm_rope(
            zq[:, c * LANES:(c + 1) * LANES], qg).astype(BF16)

    o = ATTN_WIDTH
    zkv = jnp.dot(u, w_ref[:, o:o + 2 * KV_WIDTH], preferred_element_type=F32)
    k2 = norm_rope(zkv[:, 0:KV_WIDTH], kg_ref[...])
    v2 = zkv[:, KV_WIDTH:2 * KV_WIDTH]
    o += 2 * KV_WIDTH

    a0 = jnp.where(low, k2, 0.0)
    b1 = jnp.where(low, 0.0, k2)
    k_ref[...] = jnp.concatenate(
        [a0, pltpu.roll(a0, HEAD_DIM, 1), pltpu.roll(b1, HEAD_DIM, 1), b1], axis=-1).astype(BF16)

    vt = v2.T.astype(BF16)
    ones = jnp.ones((V_ROWS - HEAD_DIM, vt.shape[1]), BF16)
    for g in range(N_KV_HEADS):
        vt_ref[0, g, 0:HEAD_DIM, :] = vt[g * HEAD_DIM:(g + 1) * HEAD_DIM, :]
        vt_ref[0, g, HEAD_DIM:V_ROWS, :] = ones
    xl_ref[...] = jnp.dot(u, w_ref[:, o:o + LRU_WIDTH], preferred_element_type=F32)
    o += LRU_WIDTH
    gl_ref[...] = jnp.dot(u, w_ref[:, o:o + LRU_WIDTH], preferred_element_type=F32)


def _inproj(x2, norm_g, w_in, cos2, sin_a, sin_b, qg, kg, bd, w_gate, w_up, batch, seq):
    t = x2.shape[0]
    tm = TM_PROJ
    steps = t // tm
    per_seq = seq // tm
    d_ff = w_gate.shape[1]
    slab = D_MODEL // steps
    row = lambda i: (i, 0)
    const = lambda i: (0, 0)
    pos = lambda i: (i % per_seq, 0)
    return pl.pallas_call(
        _inproj_kernel,
        grid=(steps,),
        in_specs=[
            pl.BlockSpec((tm, D_MODEL), row),
            pl.BlockSpec((1, D_MODEL), const),
            pl.BlockSpec((D_MODEL, D_IN), const, pipeline_mode=pl.Buffered(1)),
            pl.BlockSpec((tm, LANES), pos),
            pl.BlockSpec((tm, LANES), pos),
            pl.BlockSpec((tm, LANES), pos),
            pl.BlockSpec((1, LANES), const),
            pl.BlockSpec((1, LANES), const),
            pl.BlockSpec((2 * LANES, LANES), const),
            pl.BlockSpec((slab, d_ff), row),
            pl.BlockSpec((slab, d_ff), row),
        ],
        out_specs=[
            pl.BlockSpec((tm, ATTN_WIDTH), row),
            pl.BlockSpec((tm, 4 * LANES), row),
            pl.BlockSpec((1, N_KV_HEADS, V_ROWS, tm), lambda i: (i // per_seq, 0, 0, i % per_seq)),
            pl.BlockSpec((tm, LRU_WIDTH), row),
            pl.BlockSpec((tm, LRU_WIDTH), row),
            pl.BlockSpec((slab, d_ff), row),
            pl.BlockSpec((slab, d_ff), row),
        ],
        out_shape=[
            jax.ShapeDtypeStruct((t, ATTN_WIDTH), BF16),
            jax.ShapeDtypeStruct((t, 4 * LANES), BF16),
            jax.ShapeDtypeStruct((batch, N_KV_HEADS, V_ROWS, seq), BF16),
            jax.ShapeDtypeStruct((t, LRU_WIDTH), F32),
            jax.ShapeDtypeStruct((t, LRU_WIDTH), F32),
            jax.ShapeDtypeStruct((D_MODEL, d_ff), BF16),
            jax.ShapeDtypeStruct((D_MODEL, d_ff), BF16),
        ],
        scratch_shapes=[pltpu.VMEM((D_MODEL, D_IN), BF16)],
        compiler_params=_params(1),
        name="inproj",
    )(x2, norm_g, w_in, cos2, sin_a, sin_b, qg, kg, bd, w_gate, w_up)


def _attn_kernel(bound_ref, q_ref, k_ref, vt_ref, g_ref, o_ref, ot_ref, s0_ref, s1_ref):
    nt = (((1,), (1,)), ((), ()))
    group = N_Q_HEADS // N_KV_HEADS
    s_slots = (s0_ref, s1_ref)

    def scores(h):
        col = (2 * (h // group) + h % 2) * LANES
        return lax.dot_general(k_ref[:, col:col + LANES], q_ref[:, (h // 2) * LANES:(h // 2 + 1) * LANES],
                               nt, preferred_element_type=F32)

    def weighted_values(h, p):
        ot = jnp.dot(vt_ref[0, h // group], p, preferred_element_type=F32)
        inv = 1.0 / ot[HEAD_DIM:HEAD_DIM + 1, :]
        ot_ref[h * HEAD_DIM:(h + 1) * HEAD_DIM, :] = ot[0:HEAD_DIM, :] * inv

    bound = bound_ref[0, 0]

    @pl.when(bound <= SHIFT_BOUND_MAX)
    def _():
        for h in range(N_Q_HEADS):
            weighted_values(h, jnp.exp2(scores(h) - bound).astype(BF16))

    @pl.when(bound > SHIFT_BOUND_MAX)
    def _():
        def stage_scores(h):
            s = scores(h)
            s_slots[h % 2][...] = s
            return jnp.max(s, axis=0, keepdims=True)

        m = stage_scores(0)
        for h in range(N_Q_HEADS):
            m_next = stage_scores(h + 1) if h + 1 < N_Q_HEADS else None
            weighted_values(h, jnp.exp2(s_slots[h % 2][...] - m).astype(BF16))
            m = m_next

    ot = ot_ref[...]
    scale = lax.rsqrt(jnp.mean(ot * ot, axis=0, keepdims=True) + EPS)
    for c in range(ot.shape[1] // LANES):
        cols = slice(c * LANES, (c + 1) * LANES)
        o_ref[0, :, cols] = (ot[:, cols] * scale[:, cols] * g_ref[...]).astype(BF16)


def _attention(bound, q, k, vt, gain, batch, seq):
    t = q.shape[0]
    nq = seq // TQ
    return pl.pallas_call(
        _attn_kernel,
        grid=(batch, nq),
        in_specs=[
            pl.BlockSpec(memory_space=pltpu.SMEM),
            pl.BlockSpec((TQ, ATTN_WIDTH), lambda b, i: (b * nq + i, 0)),
            pl.BlockSpec((seq, 4 * LANES), lambda b, i: (b, 0)),
            pl.BlockSpec((1, N_KV_HEADS, V_ROWS, seq), lambda b, i: (b, 0, 0, 0)),
            pl.BlockSpec((ATTN_WIDTH, LANES), lambda b, i: (0, 0)),
        ],
        out_specs=pl.BlockSpec((1, ATTN_WIDTH, TQ), lambda b, i: (b, 0, i)),
        out_shape=jax.ShapeDtypeStruct((batch, ATTN_WIDTH, seq), BF16),
        scratch_shapes=[pltpu.VMEM((ATTN_WIDTH, TQ), F32), pltpu.VMEM((seq, TQ), F32),
                        pltpu.VMEM((seq, TQ), F32)],
        compiler_params=_params(2),
        name="attention",
    )(bound, q, k, vt, gain)


def _lru_kernel(xl_ref, gl_ref, cw_ref, cb_ref, wg_ref, br_ref, bi_ref, lam_ref, g_ref,
                wdown32_ref, wout32_ref, o_ref, wdown_ref, wout_ref,
                xcol, a_f, b_f, a_b, b_b, cf_ref, cbk_ref):
    wdown_ref[...] = wdown32_ref[...].astype(BF16)
    wout_ref[...] = wout32_ref[...].astype(BF16)

    seq = xl_ref.shape[0]
    seg = seq // N_SEG
    sst = seg + SEG_SKEW
    n_col = LRU_WIDTH // LANES
    half_cols = n_col // 2
    half_w = LRU_WIDTH // 2
    pad = SUBLANES

    zeros_pad = jnp.zeros((pad, LANES), F32)
    for c in range(n_col):
        xcol[c, 0:pad, :] = zeros_pad
        xcol[c, pad + seq:pad + seq + pad, :] = zeros_pad
        xcol[c, pad:pad + seq, :] = xl_ref[:, c * LANES:(c + 1) * LANES]

    lam = lam_ref[...]
    c_sp = LRU_C * (jnp.maximum(-lam, 0.0) + jnp.log1p(jnp.exp(-jnp.abs(lam))))
    cw = cw_ref[...]
    cbias = cb_ref[...]
    br = br_ref[...]
    bi = bi_ref[...]
    a_scr = (a_f, a_b)
    b_scr = (b_f, b_b)

    def gate_chunk(ci, _):
        r0 = pl.multiple_of(ci * GATE_ROWS, GATE_ROWS)
        dst = pl.ds(r0 + (r0 // seg) * SEG_SKEW, GATE_ROWS)
        xc = []
        for c in range(n_col):
            cs = slice(c * LANES, (c + 1) * LANES)
            acc = cbias[:, cs]
            for k in range(CONV_WIDTH):
                acc = acc + xcol[c, pl.ds(r0 + pad - 1 + k, GATE_ROWS), :] * cw[k:k + 1, cs]
            xc.append(acc)
        for h in range(2):
            lhs = jnp.concatenate(xc[h * half_cols:(h + 1) * half_cols], axis=-1).astype(BF16)
            z = jnp.dot(lhs, wg_ref[h], preferred_element_type=F32)
            for cc in range(half_cols):
                c = h * half_cols + cc
                cs = slice(c * LANES, (c + 1) * LANES)
                for d in range(2):
                    zo = 2 * d * half_w + cc * LANES
                    r = jax.nn.sigmoid(z[:, zo:zo + LANES] + br[d:d + 1, cs])
                    i = jax.nn.sigmoid(z[:, zo + half_w:zo + half_w + LANES] + bi[d:d + 1, cs])
                    w = c_sp[d:d + 1, cs] * r
                    a = jnp.exp2(w * (-LOG2E))
                    v = jnp.tanh(w) * (a * a + 1.0)
                    mult = jnp.where(v > 0.0, v * lax.rsqrt(v), 0.0)
                    a_scr[d][c, dst, :] = a
                    b_scr[d][c, dst, :] = mult * (i * xc[c])
        return 0

    lax.fori_loop(0, seq // GATE_ROWS, gate_chunk, 0)

    def scan_step(t, carry):
        hf, pf, hb, pb = carry
        idx_f = pl.ds(t, N_SEG, stride=sst)
        idx_b = pl.ds(seg - 1 - t, N_SEG, stride=sst)
        nhf, npf, nhb, npb = [], [], [], []
        for c in range(n_col):
            af = a_f[c, idx_f, :]
            h = af * hf[c] + b_f[c, idx_f, :]
            p = af * pf[c]
            b_f[c, idx_f, :] = h
            a_f[c, idx_f, :] = p
            nhf.append(h)
            npf.append(p)
            ab = a_b[c, idx_b, :]
            h = ab * hb[c] + b_b[c, idx_b, :]
            p = ab * pb[c]
            b_b[c, idx_b, :] = h
            a_b[c, idx_b, :] = p
            nhb.append(h)
            npb.append(p)
        return tuple(nhf), tuple(npf), tuple(nhb), tuple(npb)

    zero = tuple(jnp.zeros((N_SEG, LANES), F32) for _ in range(n_col))
    one = tuple(jnp.ones((N_SEG, LANES), F32) for _ in range(n_col))
    hf, pf, hb, pb = lax.fori_loop(0, seg, scan_step, (zero, one, zero, one), unroll=4)

    for c in range(n_col):
        cin = jnp.zeros((1, LANES), F32)
        for j in range(N_SEG):
            cf_ref[c, j:j + 1, :] = cin
            cin = hf[c][j:j + 1, :] + pf[c][j:j + 1, :] * cin
        cin = jnp.zeros((1, LANES), F32)
        for j in range(N_SEG - 1, -1, -1):
            cbk_ref[c, j:j + 1, :] = cin
            cin = hb[c][j:j + 1, :] + pb[c][j:j + 1, :] * cin

    gain = g_ref[...]

    def out_chunk(ci, _):
        r0 = pl.multiple_of(ci * OUT_ROWS, OUT_ROWS)
        rows = pl.ds(r0, OUT_ROWS)
        j = r0 // seg
        src = pl.ds(r0 + j * SEG_SKEW, OUT_ROWS)
        y = jnp.concatenate(
            [b_f[c, src, :] + a_f[c, src, :] * cf_ref[c, pl.ds(j, 1), :]
             + b_b[c, src, :] + a_b[c, src, :] * cbk_ref[c, pl.ds(j, 1), :]
             for c in range(n_col)], axis=-1)
        out = y * jax.nn.gelu(gl_ref[rows, :])
        o_ref[rows, :] = _rms(out, gain).astype(BF16)
        return 0

    lax.fori_loop(0, seq // OUT_ROWS, out_chunk, 0)


def _lru(xl, gl, conv_w, conv_b, wg, b_r, b_i, lam, gain, w_down, w_out, batch, seq):
    t = xl.shape[0]
    d_ff = w_down.shape[0]
    row = lambda b: (b, 0)
    const2 = lambda b: (0, 0)
    return pl.pallas_call(
        _lru_kernel,
        grid=(batch,),
        in_specs=[
            pl.BlockSpec((seq, LRU_WIDTH), row),
            pl.BlockSpec((seq, LRU_WIDTH), row),
            pl.BlockSpec((CONV_WIDTH, LRU_WIDTH), const2),
            pl.BlockSpec((1, LRU_WIDTH), const2),
            pl.BlockSpec((2, LRU_WIDTH // 2, 2 * LRU_WIDTH), lambda b: (0, 0, 0)),
            pl.BlockSpec((2, LRU_WIDTH), const2),
            pl.BlockSpec((2, LRU_WIDTH), const2),
            pl.BlockSpec((2, LRU_WIDTH), const2),
            pl.BlockSpec((1, LRU_WIDTH), const2),
            pl.BlockSpec((d_ff // batch, D_MODEL), row),
            pl.BlockSpec((D_MODEL // batch, D_MODEL), row),
        ],
        out_specs=[
            pl.BlockSpec((seq, LRU_WIDTH), row),
            pl.BlockSpec((d_ff // batch, D_MODEL), row),
            pl.BlockSpec((D_MODEL // batch, D_MODEL), row),
        ],
        out_shape=[
            jax.ShapeDtypeStruct((t, LRU_WIDTH), BF16),
            jax.ShapeDtypeStruct((d_ff, D_MODEL), BF16),
            jax.ShapeDtypeStruct((D_MODEL, D_MODEL), BF16),
        ],
        scratch_shapes=[
            pltpu.VMEM((LRU_WIDTH // LANES, seq + 2 * SUBLANES, LANES), F32),
            pltpu.VMEM((LRU_WIDTH // LANES, seq + N_SEG * SEG_SKEW, LANES), F32),
            pltpu.VMEM((LRU_WIDTH // LANES, seq + N_SEG * SEG_SKEW, LANES), F32),
            pltpu.VMEM((LRU_WIDTH // LANES, seq + N_SEG * SEG_SKEW, LANES), F32),
            pltpu.VMEM((LRU_WIDTH // LANES, seq + N_SEG * SEG_SKEW, LANES), F32),
            pltpu.VMEM((LRU_WIDTH // LANES, N_SEG, LANES), F32),
            pltpu.VMEM((LRU_WIDTH // LANES, N_SEG, LANES), F32),
        ],
        compiler_params=_params(1),
        name="rglru",
    )(xl, gl, conv_w, conv_b, wg, b_r, b_i, lam, gain, w_down, w_out)


def _ffn_kernel(x_ref, at_ref, lr_ref, wo_ref, g_ref, wg_ref, wu_ref, wd_ref, o_ref, ff_ref):
    tn = (((0,), (0,)), ((), ()))
    h1 = (x_ref[...]
          + lax.dot_general(at_ref[0], wo_ref[0:ATTN_WIDTH, :], tn, preferred_element_type=F32)
          + jnp.dot(lr_ref[...], wo_ref[ATTN_WIDTH:, :], preferred_element_type=F32))
    u = _rms(h1, g_ref[...]).astype(BF16)
    o_ref[...] = h1

    for c in range(wg_ref.shape[1] // FF_CHUNK):
        cols = slice(c * FF_CHUNK, (c + 1) * FF_CHUNK)
        gate = jnp.dot(u, wg_ref[:, cols], preferred_element_type=F32)
        up = jnp.dot(u, wu_ref[:, cols], preferred_element_type=F32)
        ff_ref[:, cols] = (jax.nn.silu(gate) * up).astype(BF16)

    o_ref[...] += jnp.dot(ff_ref[...], wd_ref[...], preferred_element_type=F32)


def _ffn(x2, attn_nt, lru_n, w_out, norm_g, wg, wu, wd):
    t = x2.shape[0]
    tm = TM_FFN
    d_ff = wg.shape[1]
    per_seq = attn_nt.shape[2] // tm
    row = lambda i: (i, 0)
    const2 = lambda i: (0, 0)
    once = pl.Buffered(1)
    return pl.pallas_call(
        _ffn_kernel,
        grid=(t // tm,),
        in_specs=[
            pl.BlockSpec((tm, D_MODEL), row),
            pl.BlockSpec((1, ATTN_WIDTH, tm), lambda i: (i // per_seq, 0, i % per_seq)),
            pl.BlockSpec((tm, LRU_WIDTH), row),
            pl.BlockSpec((D_MODEL, D_MODEL), const2, pipeline_mode=once),
            pl.BlockSpec((1, D_MODEL), const2),
            pl.BlockSpec((D_MODEL, d_ff), const2, pipeline_mode=once),
            pl.BlockSpec((D_MODEL, d_ff), const2, pipeline_mode=once),
            pl.BlockSpec((d_ff, D_MODEL), const2, pipeline_mode=once),
        ],
        out_specs=pl.BlockSpec((tm, D_MODEL), row),
        out_shape=jax.ShapeDtypeStruct((t, D_MODEL), F32),
        scratch_shapes=[pltpu.VMEM((tm, d_ff), BF16)],
        compiler_params=_params(1),
        name="outproj_ffn",
    )(x2, attn_nt, lru_n, w_out, norm_g, wg, wu, wd)


def _rope_tables(seq):
    f32 = np.float32
    rows = seq // GRID_W
    row = np.repeat(np.arange(rows, dtype=f32), GRID_W)
    col = np.tile(np.arange(GRID_W, dtype=f32), rows)
    axis_dim = HEAD_DIM // 2
    inv = np.power(f32(ROPE_THETA), -np.arange(0, axis_dim, 2, dtype=f32) / f32(axis_dim)).astype(f32)
    ang_r = row[:, None] * inv[None, :]
    ang_c = col[:, None] * inv[None, :]
    ang = np.concatenate([ang_r, ang_r, ang_c, ang_c], axis=-1)
    ang = np.concatenate([ang, ang], axis=-1)
    first = (np.arange(LANES) % axis_dim) < (axis_dim // 2)
    cos = np.cos(ang).astype(f32)
    sin = np.sin(ang).astype(f32)
    sin_a = np.where(first[None, :], -sin, f32(0.0)).astype(f32)
    sin_b = np.where(first[None, :], f32(0.0), sin).astype(f32)
    return jnp.asarray(cos), jnp.asarray(sin_a), jnp.asarray(sin_b)


def _block_diag_halves(w):
    per = (LRU_WIDTH // 2) // LRU_BLOCK
    blocks = w.reshape(2, per, LRU_BLOCK, LRU_BLOCK)
    eye = jnp.eye(per, dtype=w.dtype)
    dense = blocks[:, :, :, None, :] * eye[None, :, None, :, None]
    return dense.reshape(2, LRU_WIDTH // 2, LRU_WIDTH // 2)


def kernel(x, norm_mix, w_in, q_norm, k_norm, conv_w, conv_b, w_rgate, b_rgate, w_igate, b_igate, lru_lambda, out_norm_attn, out_norm_lru, w_out, norm_ffn, w_gate, w_up, w_down):
    batch, seq, _ = x.shape
    depth = norm_mix.shape[0]
    cos2, sin_a, sin_b = _rope_tables(seq)
    head_id = jnp.arange(LANES) // HEAD_DIM
    bd = (head_id[:, None] == head_id[None, :]).astype(BF16)
    bd = jnp.concatenate([bd, bd], axis=0)

    h = x.reshape(batch * seq, D_MODEL)
    for l in range(depth):
        qg = jnp.tile(q_norm[l] * (HEAD_DIM ** -0.5 * LOG2E), 2)[None, :]
        kg = jnp.tile(k_norm[l], 2)[None, :]
        q, k, vt, xl, gl, w_gate16, w_up16 = _inproj(
            h, norm_mix[l][None, :], w_in[l], cos2, sin_a, sin_b, qg, kg, bd,
            w_gate[l], w_up[l], batch, seq)
        bound = (BOUND_MARGIN * HEAD_DIM * (HEAD_DIM ** -0.5 * LOG2E)
                 * jnp.max(jnp.abs(q_norm[l])) * jnp.max(jnp.abs(k_norm[l]))).reshape(1, 1)
        attn_gain = jnp.broadcast_to(out_norm_attn[l][:, None], (ATTN_WIDTH, LANES))
        attn_nt = _attention(bound, q, k, vt, attn_gain, batch, seq)

        wg = jnp.concatenate([_block_diag_halves(w_rgate[l, 0]), _block_diag_halves(w_igate[l, 0]),
                              _block_diag_halves(w_rgate[l, 1]), _block_diag_halves(w_igate[l, 1])],
                             axis=2)
        wg = wg.astype(BF16)
        lru_n, w_down16, w_out16 = _lru(
            xl, gl, conv_w[l], conv_b[l][None, :], wg, b_rgate[l], b_igate[l],
            lru_lambda[l], out_norm_lru[l][None, :], w_down[l], w_out[l], batch, seq)

        h = _ffn(h, attn_nt, lru_n, w_out16, norm_ffn[l][None, :], w_gate16, w_up16, w_down16)
    return h.reshape(batch, seq, D_MODEL)
```

```python
import jax
import jax.numpy as jnp
import numpy as np
from jax import lax
from jax.experimental import pallas as pl
from jax.experimental.pallas import tpu as pltpu

F32 = jnp.float32
BF16 = jnp.bfloat16

D_MODEL = 1024
GRID_W = 64
HEAD_DIM = 64
N_Q_HEADS = 8
N_KV_HEADS = 2
ATTN_WIDTH = N_Q_HEADS * HEAD_DIM
KV_WIDTH = N_KV_HEADS * HEAD_DIM
LRU_WIDTH = D_MODEL - ATTN_WIDTH
LRU_BLOCKS = 8
LRU_BLOCK = LRU_WIDTH // LRU_BLOCKS
CONV_WIDTH = 4
LRU_C = 8.0
D_IN = ATTN_WIDTH + 2 * KV_WIDTH + 2 * LRU_WIDTH
ROPE_THETA = 10000.0
EPS = 1e-6

LANES = 128
SUBLANES = 8
MXU_DIM = 256
VMEM_LIMIT = 56 * 1024 * 1024

TM_PROJ = 1024
TM_FFN = 1024
TQ = 512
SHIFT_BOUND_MAX = 40.0
BOUND_MARGIN = 1.02
FF_CHUNK = MXU_DIM
GATE_ROWS = 256
OUT_ROWS = 256
N_SEG = SUBLANES
SEG_SKEW = 1
LOG2E = 1.4426950408889634
V_ROWS = HEAD_DIM + 16


def _params(n_grid_axes):
    return pltpu.CompilerParams(dimension_semantics=("arbitrary",) * n_grid_axes,
                                vmem_limit_bytes=VMEM_LIMIT)


def _rms(x, gain):
    return x * lax.rsqrt(jnp.mean(x * x, axis=-1, keepdims=True) + EPS) * gain


def _inproj_kernel(x_ref, g_ref, w32_ref, cos_ref, sa_ref, sb_ref, qg_ref, kg_ref, bd_ref,
                   wgate32_ref, wup32_ref,
                   q_ref, k_ref, vt_ref, xl_ref, gl_ref, wgate_ref, wup_ref, w_ref):
    @pl.when(pl.program_id(0) == 0)
    def _():
        w_ref[...] = w32_ref[...].astype(BF16)

    wgate_ref[...] = wgate32_ref[...].astype(BF16)
    wup_ref[...] = wup32_ref[...].astype(BF16)

    u = _rms(x_ref[...], g_ref[...]).astype(BF16)
    cos = cos_ref[...]
    sin_a = sa_ref[...]
    sin_b = sb_ref[...]
    bd = bd_ref[...]
    lane = lax.broadcasted_iota(jnp.int32, (1, LANES), 1)
    low = lane < HEAD_DIM

    def norm_rope(z, gain):
        sq = z * z
        hi = sq.astype(BF16)
        lo = (sq - hi.astype(F32)).astype(BF16)
        ss = jnp.dot(jnp.concatenate([hi, lo], axis=-1), bd, preferred_element_type=F32)
        y = z * lax.rsqrt(ss * (1.0 / HEAD_DIM) + EPS) * gain
        return (y * cos + pltpu.roll(y, LANES - 16, 1) * sin_a
                + pltpu.roll(y, 16, 1) * sin_b)

    qg = qg_ref[...]
    zq = jnp.dot(u, w_ref[:, 0:ATTN_WIDTH], preferred_element_type=F32)
    for c in range(ATTN_WIDTH // LANES):
        q_ref[:, c * LANES:(c + 1) * LANES] = norm_rope(
            zq[:, c * LANES:(c + 1) * LANES], qg).astype(BF16)

    o = ATTN_WIDTH
    zkv = jnp.dot(u, w_ref[:, o:o + 2 * KV_WIDTH], preferred_element_type=F32)
    k2 = norm_rope(zkv[:, 0:KV_WIDTH], kg_ref[...])
    v2 = zkv[:, KV_WIDTH:2 * KV_WIDTH]
    o += 2 * KV_WIDTH

    a0 = jnp.where(low, k2, 0.0)
    b1 = jnp.where(low, 0.0, k2)
    k_ref[...] = jnp.concatenate(
        [a0, pltpu.roll(a0, HEAD_DIM, 1), pltpu.roll(b1, HEAD_DIM, 1), b1], axis=-1).astype(BF16)

    vt = v2.T.astype(BF16)
    ones = jnp.ones((V_ROWS - HEAD_DIM, vt.shape[1]), BF16)
    for g in range(N_KV_HEADS):
        vt_ref[0, g, 0:HEAD_DIM, :] = vt[g * HEAD_DIM:(g + 1) * HEAD_DIM, :]
        vt_ref[0, g, HEAD_DIM:V_ROWS, :] = ones
    gl_ref[...] = jax.nn.gelu(
        jnp.dot(u, w_ref[:, o + LRU_WIDTH:o + 2 * LRU_WIDTH], preferred_element_type=F32))
    xl_ref[...] = jnp.dot(u, w_ref[:, o:o + LRU_WIDTH], preferred_element_type=F32)


def _inproj(x2, norm_g, w_in, cos2, sin_a, sin_b, qg, kg, bd, w_gate, w_up, batch, seq):
    t = x2.shape[0]
    tm = TM_PROJ
    steps = t // tm
    per_seq = seq // tm
    d_ff = w_gate.shape[1]
    slab = D_MODEL // steps
    row = lambda i: (i, 0)
    const = lambda i: (0, 0)
    pos = lambda i: (i % per_seq, 0)
    return pl.pallas_call(
        _inproj_kernel,
        grid=(steps,),
        in_specs=[
            pl.BlockSpec((tm, D_MODEL), row),
            pl.BlockSpec((1, D_MODEL), const),
            pl.BlockSpec((D_MODEL, D_IN), const, pipeline_mode=pl.Buffered(1)),
            pl.BlockSpec((tm, LANES), pos),
            pl.BlockSpec((tm, LANES), pos),
            pl.BlockSpec((tm, LANES), pos),
            pl.BlockSpec((1, LANES), const),
            pl.BlockSpec((1, LANES), const),
            pl.BlockSpec((2 * LANES, LANES), const),
            pl.BlockSpec((slab, d_ff), row),
            pl.BlockSpec((slab, d_ff), row),
        ],
        out_specs=[
            pl.BlockSpec((tm, ATTN_WIDTH), row),
            pl.BlockSpec((tm, 4 * LANES), row),
            pl.BlockSpec((1, N_KV_HEADS, V_ROWS, tm), lambda i: (i // per_seq, 0, 0, i % per_seq)),
            pl.BlockSpec((tm, LRU_WIDTH), row),
            pl.BlockSpec((tm, LRU_WIDTH), row),
            pl.BlockSpec((slab, d_ff), row),
            pl.BlockSpec((slab, d_ff), row),
        ],
        out_shape=[
            jax.ShapeDtypeStruct((t, ATTN_WIDTH), BF16),
            jax.ShapeDtypeStruct((t, 4 * LANES), BF16),
            jax.ShapeDtypeStruct((batch, N_KV_HEADS, V_ROWS, seq), BF16),
            jax.ShapeDtypeStruct((t, LRU_WIDTH), F32),
            jax.ShapeDtypeStruct((t, LRU_WIDTH), F32),
            jax.ShapeDtypeStruct((D_MODEL, d_ff), BF16),
            jax.ShapeDtypeStruct((D_MODEL, d_ff), BF16),
        ],
        scratch_shapes=[pltpu.VMEM((D_MODEL, D_IN), BF16)],
        compiler_params=_params(1),
        name="inproj",
    )(x2, norm_g, w_in, cos2, sin_a, sin_b, qg, kg, bd, w_gate, w_up)


def _attn_kernel(bound_ref, q_ref, k_ref, vt_ref, g_ref, o_ref, ot_ref, s0_ref, s1_ref):
    nt = (((1,), (1,)), ((), ()))
    group = N_Q_HEADS // N_KV_HEADS
    s_slots = (s0_ref, s1_ref)

    def scores(h):
        col = (2 * (h // group) + h % 2) * LANES
        return lax.dot_general(k_ref[:, col:col + LANES], q_ref[:, (h // 2) * LANES:(h // 2 + 1) * LANES],
                               nt, preferred_element_type=F32)

    def weighted_values(h, p):
        ot = jnp.dot(vt_ref[0, h // group], p, preferred_element_type=F32)
        inv = 1.0 / ot[HEAD_DIM:HEAD_DIM + 1, :]
        ot_ref[h * HEAD_DIM:(h + 1) * HEAD_DIM, :] = ot[0:HEAD_DIM, :] * inv

    bound = bound_ref[0, 0]

    @pl.when(bound <= SHIFT_BOUND_MAX)
    def _():
        for h in range(N_Q_HEADS):
            weighted_values(h, jnp.exp2(scores(h) - bound).astype(BF16))

    @pl.when(bound > SHIFT_BOUND_MAX)
    def _():
        def stage_scores(h):
            s = scores(h)
            s_slots[h % 2][...] = s
            return jnp.max(s, axis=0, keepdims=True)

        m = stage_scores(0)
        for h in range(N_Q_HEADS):
            m_next = stage_scores(h + 1) if h + 1 < N_Q_HEADS else None
            weighted_values(h, jnp.exp2(s_slots[h % 2][...] - m).astype(BF16))
            m = m_next

    ot = ot_ref[...]
    scale = lax.rsqrt(jnp.mean(ot * ot, axis=0, keepdims=True) + EPS)
    for c in range(ot.shape[1] // LANES):
        cols = slice(c * LANES, (c + 1) * LANES)
        o_ref[0, :, cols] = (ot[:, cols] * scale[:, cols] * g_ref[...]).astype(BF16)


def _attention(bound, q, k, vt, gain, batch, seq):
    t = q.shape[0]
    nq = seq // TQ
    return pl.pallas_call(
        _attn_kernel,
        grid=(batch, nq),
        in_specs=[
            pl.BlockSpec(memory_space=pltpu.SMEM),
            pl.BlockSpec((TQ, ATTN_WIDTH), lambda b, i: (b * nq + i, 0)),
            pl.BlockSpec((seq, 4 * LANES), lambda b, i: (b, 0)),
            pl.BlockSpec((1, N_KV_HEADS, V_ROWS, seq), lambda b, i: (b, 0, 0, 0)),
            pl.BlockSpec((ATTN_WIDTH, LANES), lambda b, i: (0, 0)),
        ],
        out_specs=pl.BlockSpec((1, ATTN_WIDTH, TQ), lambda b, i: (b, 0, i)),
        out_shape=jax.ShapeDtypeStruct((batch, ATTN_WIDTH, seq), BF16),
        scratch_shapes=[pltpu.VMEM((ATTN_WIDTH, TQ), F32), pltpu.VMEM((seq, TQ), F32),
                        pltpu.VMEM((seq, TQ), F32)],
        compiler_params=_params(2),
        name="attention",
    )(bound, q, k, vt, gain)


def _lru_kernel(xl_ref, gl_ref, cw_ref, cb_ref, wg_ref, br_ref, bi_ref, lam_ref, g_ref,
                wdown32_ref, wout32_ref, o_ref, wdown_ref, wout_ref,
                xcol, a_f, b_f, a_b, b_b, cf_ref, cbk_ref):
    wdown_ref[...] = wdown32_ref[...].astype(BF16)
    wout_ref[...] = wout32_ref[...].astype(BF16)

    seq = xl_ref.shape[0]
    seg = seq // N_SEG
    sst = seg + SEG_SKEW
    n_col = LRU_WIDTH // LANES
    half_cols = n_col // 2
    half_w = LRU_WIDTH // 2
    pad = SUBLANES

    zeros_pad = jnp.zeros((pad, LANES), F32)
    for c in range(n_col):
        xcol[c, 0:pad, :] = zeros_pad
        xcol[c, pad + seq:pad + seq + pad, :] = zeros_pad
        xcol[c, pad:pad + seq, :] = xl_ref[:, c * LANES:(c + 1) * LANES]

    lam = lam_ref[...]
    c_sp = LRU_C * (jnp.maximum(-lam, 0.0) + jnp.log1p(jnp.exp(-jnp.abs(lam))))
    cw = cw_ref[...]
    cbias = cb_ref[...]
    br = br_ref[...]
    bi = bi_ref[...]
    a_scr = (a_f, a_b)
    b_scr = (b_f, b_b)

    def gate_chunk(ci, _):
        r0 = pl.multiple_of(ci * GATE_ROWS, GATE_ROWS)
        dst = pl.ds(r0 + (r0 // seg) * SEG_SKEW, GATE_ROWS)
        xc = []
        for c in range(n_col):
            cs = slice(c * LANES, (c + 1) * LANES)
            acc = cbias[:, cs]
            for k in range(CONV_WIDTH):
                acc = acc + xcol[c, pl.ds(r0 + pad - 1 + k, GATE_ROWS), :] * cw[k:k + 1, cs]
            xc.append(acc)
        for h in range(2):
            lhs = jnp.concatenate(xc[h * half_cols:(h + 1) * half_cols], axis=-1).astype(BF16)
            z = jnp.dot(lhs, wg_ref[h], preferred_element_type=F32)
            for cc in range(half_cols):
                c = h * half_cols + cc
                cs = slice(c * LANES, (c + 1) * LANES)
                for d in range(2):
                    zo = 2 * d * half_w + cc * LANES
                    r = jax.nn.sigmoid(z[:, zo:zo + LANES] + br[d:d + 1, cs])
                    i = jax.nn.sigmoid(z[:, zo + half_w:zo + half_w + LANES] + bi[d:d + 1, cs])
                    w = c_sp[d:d + 1, cs] * r
                    a = jnp.exp2(w * (-LOG2E))
                    v = jnp.tanh(w) * (a * a + 1.0)
                    mult = jnp.where(v > 0.0, v * lax.rsqrt(v), 0.0)
                    a_scr[d][c, dst, :] = a
                    b_scr[d][c, dst, :] = mult * (i * xc[c])
        return 0

    lax.fori_loop(0, seq // GATE_ROWS, gate_chunk, 0)

    def scan_step(t, carry):
        hf, pf, hb, pb = carry
        idx_f = pl.ds(t, N_SEG, stride=sst)
        idx_b = pl.ds(seg - 1 - t, N_SEG, stride=sst)
        nhf, npf, nhb, npb = [], [], [], []
        for c in range(n_col):
            af = a_f[c, idx_f, :]
            h = af * hf[c] + b_f[c, idx_f, :]
            p = af * pf[c]
            b_f[c, idx_f, :] = h
            a_f[c, idx_f, :] = p
            nhf.append(h)
            npf.append(p)
            ab = a_b[c, idx_b, :]
            h = ab * hb[c] + b_b[c, idx_b, :]
            p = ab * pb[c]
            b_b[c, idx_b, :] = h
            a_b[c, idx_b, :] = p
            nhb.append(h)
            npb.append(p)
        return tuple(nhf), tuple(npf), tuple(nhb), tuple(npb)

    zero = tuple(jnp.zeros((N_SEG, LANES), F32) for _ in range(n_col))
    one = tuple(jnp.ones((N_SEG, LANES), F32) for _ in range(n_col))
    hf, pf, hb, pb = lax.fori_loop(0, seg, scan_step, (zero, one, zero, one), unroll=4)

    for c in range(n_col):
        cin = jnp.zeros((1, LANES), F32)
        for j in range(N_SEG):
            cf_ref[c, j:j + 1, :] = cin
            cin = hf[c][j:j + 1, :] + pf[c][j:j + 1, :] * cin
        cin = jnp.zeros((1, LANES), F32)
        for j in range(N_SEG - 1, -1, -1):
            cbk_ref[c, j:j + 1, :] = cin
            cin = hb[c][j:j + 1, :] + pb[c][j:j + 1, :] * cin

    gain = g_ref[...]

    def out_chunk(ci, _):
        r0 = pl.multiple_of(ci * OUT_ROWS, OUT_ROWS)
        rows = pl.ds(r0, OUT_ROWS)
        j = r0 // seg
        src = pl.ds(r0 + j * SEG_SKEW, OUT_ROWS)
        y = jnp.concatenate(
            [b_f[c, src, :] + a_f[c, src, :] * cf_ref[c, pl.ds(j, 1), :]
             + b_b[c, src, :] + a_b[c, src, :] * cbk_ref[c, pl.ds(j, 1), :]
             for c in range(n_col)], axis=-1)
        out = y * gl_ref[rows, :]
        o_ref[rows, :] = _rms(out, gain).astype(BF16)
        return 0

    lax.fori_loop(0, seq // OUT_ROWS, out_chunk, 0)


def _lru(xl, gl, conv_w, conv_b, wg, b_r, b_i, lam, gain, w_down, w_out, batch, seq):
    t = xl.shape[0]
    d_ff = w_down.shape[0]
    row = lambda b: (b, 0)
    const2 = lambda b: (0, 0)
    return pl.pallas_call(
        _lru_kernel,
        grid=(batch,),
        in_specs=[
            pl.BlockSpec((seq, LRU_WIDTH), row),
            pl.BlockSpec((seq, LRU_WIDTH), row),
            pl.BlockSpec((CONV_WIDTH, LRU_WIDTH), const2),
            pl.BlockSpec((1, LRU_WIDTH), const2),
            pl.BlockSpec((2, LRU_WIDTH // 2, 2 * LRU_WIDTH), lambda b: (0, 0, 0)),
            pl.BlockSpec((2, LRU_WIDTH), const2),
            pl.BlockSpec((2, LRU_WIDTH), const2),
            pl.BlockSpec((2, LRU_WIDTH), const2),
            pl.BlockSpec((1, LRU_WIDTH), const2),
            pl.BlockSpec((d_ff // batch, D_MODEL), row),
            pl.BlockSpec((D_MODEL // batch, D_MODEL), row),
        ],
        out_specs=[
            pl.BlockSpec((seq, LRU_WIDTH), row),
            pl.BlockSpec((d_ff // batch, D_MODEL), row),
            pl.BlockSpec((D_MODEL // batch, D_MODEL), row),
        ],
        out_shape=[
            jax.ShapeDtypeStruct((t, LRU_WIDTH), BF16),
            jax.ShapeDtypeStruct((d_ff, D_MODEL), BF16),
            jax.ShapeDtypeStruct((D_MODEL, D_MODEL), BF16),
        ],
        scratch_shapes=[
            pltpu.VMEM((LRU_WIDTH // LANES, seq + 2 * SUBLANES, LANES), F32),
            pltpu.VMEM((LRU_WIDTH // LANES, seq + N_SEG * SEG_SKEW, LANES), F32),
            pltpu.VMEM((LRU_WIDTH // LANES, seq + N_SEG * SEG_SKEW, LANES), F32),
            pltpu.VMEM((LRU_WIDTH // LANES, seq + N_SEG * SEG_SKEW, LANES), F32),
            pltpu.VMEM((LRU_WIDTH // LANES, seq + N_SEG * SEG_SKEW, LANES), F32),
            pltpu.VMEM((LRU_WIDTH // LANES, N_SEG, LANES), F32),
            pltpu.VMEM((LRU_WIDTH // LANES, N_SEG, LANES), F32),
        ],
        compiler_params=_params(1),
        name="rglru",
    )(xl, gl, conv_w, conv_b, wg, b_r, b_i, lam, gain, w_down, w_out)


def _ffn_kernel(x_ref, at_ref, lr_ref, wo_ref, g_ref, wg_ref, wu_ref, wd_ref, o_ref, ff_ref):
    tn = (((0,), (0,)), ((), ()))
    h1 = (x_ref[...]
          + lax.dot_general(at_ref[0], wo_ref[0:ATTN_WIDTH, :], tn, preferred_element_type=F32)
          + jnp.dot(lr_ref[...], wo_ref[ATTN_WIDTH:, :], preferred_element_type=F32))
    u = _rms(h1, g_ref[...]).astype(BF16)
    o_ref[...] = h1

    for c in range(wg_ref.shape[1] // FF_CHUNK):
        cols = slice(c * FF_CHUNK, (c + 1) * FF_CHUNK)
        gate = jnp.dot(u, wg_ref[:, cols], preferred_element_type=F32)
        up = jnp.dot(u, wu_ref[:, cols], preferred_element_type=F32)
        ff_ref[:, cols] = (jax.nn.silu(gate) * up).astype(BF16)

    o_ref[...] += jnp.dot(ff_ref[...], wd_ref[...], preferred_element_type=F32)


def _ffn(x2, attn_nt, lru_n, w_out, norm_g, wg, wu, wd):
    t = x2.shape[0]
    tm = TM_FFN
    d_ff = wg.shape[1]
    per_seq = attn_nt.shape[2] // tm
    row = lambda i: (i, 0)
    const2 = lambda i: (0, 0)
    once = pl.Buffered(1)
    return pl.pallas_call(
        _ffn_kernel,
        grid=(t // tm,),
        in_specs=[
            pl.BlockSpec((tm, D_MODEL), row),
            pl.BlockSpec((1, ATTN_WIDTH, tm), lambda i: (i // per_seq, 0, i % per_seq)),
            pl.BlockSpec((tm, LRU_WIDTH), row),
            pl.BlockSpec((D_MODEL, D_MODEL), const2, pipeline_mode=once),
            pl.BlockSpec((1, D_MODEL), const2),
            pl.BlockSpec((D_MODEL, d_ff), const2, pipeline_mode=once),
            pl.BlockSpec((D_MODEL, d_ff), const2, pipeline_mode=once),
            pl.BlockSpec((d_ff, D_MODEL), const2, pipeline_mode=once),
        ],
        out_specs=pl.BlockSpec((tm, D_MODEL), row),
        out_shape=jax.ShapeDtypeStruct((t, D_MODEL), F32),
        scratch_shapes=[pltpu.VMEM((tm, d_ff), BF16)],
        compiler_params=_params(1),
        name="outproj_ffn",
    )(x2, attn_nt, lru_n, w_out, norm_g, wg, wu, wd)


def _rope_tables(seq):
    f32 = np.float32
    rows = seq // GRID_W
    row = np.repeat(np.arange(rows, dtype=f32), GRID_W)
    col = np.tile(np.arange(GRID_W, dtype=f32), rows)
    axis_dim = HEAD_DIM // 2
    inv = np.power(f32(ROPE_THETA), -np.arange(0, axis_dim, 2, dtype=f32) / f32(axis_dim)).astype(f32)
    ang_r = row[:, None] * inv[None, :]
    ang_c = col[:, None] * inv[None, :]
    ang = np.concatenate([ang_r, ang_r, ang_c, ang_c], axis=-1)
    ang = np.concatenate([ang, ang], axis=-1)
    first = (np.arange(LANES) % axis_dim) < (axis_dim // 2)
    cos = np.cos(ang).astype(f32)
    sin = np.sin(ang).astype(f32)
    sin_a = np.where(first[None, :], -sin, f32(0.0)).astype(f32)
    sin_b = np.where(first[None, :], f32(0.0), sin).astype(f32)
    return jnp.asarray(cos), jnp.asarray(sin_a), jnp.asarray(sin_b)


def _block_diag_halves(w):
    per = (LRU_WIDTH // 2) // LRU_BLOCK
    blocks = w.reshape(2, per, LRU_BLOCK, LRU_BLOCK)
    eye = jnp.eye(per, dtype=w.dtype)
    dense = blocks[:, :, :, None, :] * eye[None, :, None, :, None]
    return dense.reshape(2, LRU_WIDTH // 2, LRU_WIDTH // 2)


def kernel(x, norm_mix, w_in, q_norm, k_norm, conv_w, conv_b, w_rgate, b_rgate, w_igate, b_igate, lru_lambda, out_norm_attn, out_norm_lru, w_out, norm_ffn, w_gate, w_up, w_down):
    batch, seq, _ = x.shape
    depth = norm_mix.shape[0]
    cos2, sin_a, sin_b = _rope_tables(seq)
    head_id = jnp.arange(LANES) // HEAD_DIM
    bd = (head_id[:, None] == head_id[None, :]).astype(BF16)
    bd = jnp.concatenate([bd, bd], axis=0)

    h = x.reshape(batch * seq, D_MODEL)
    for l in range(depth):
        qg = jnp.tile(q_norm[l] * (HEAD_DIM ** -0.5 * LOG2E), 2)[None, :]
        kg = jnp.tile(k_norm[l], 2)[None, :]
        q, k, vt, xl, gl, w_gate16, w_up16 = _inproj(
            h, norm_mix[l][None, :], w_in[l], cos2, sin_a, sin_b, qg, kg, bd,
            w_gate[l], w_up[l], batch, seq)
        bound = (BOUND_MARGIN * HEAD_DIM * (HEAD_DIM ** -0.5 * LOG2E)
                 * jnp.max(jnp.abs(q_norm[l])) * jnp.max(jnp.abs(k_norm[l]))).reshape(1, 1)
        attn_gain = jnp.broadcast_to(out_norm_attn[l][:, None], (ATTN_WIDTH, LANES))
        attn_nt = _attention(bound, q, k, vt, attn_gain, batch, seq)

        wg = jnp.concatenate([_block_diag_halves(w_rgate[l, 0]), _block_diag_halves(w_igate[l, 0]),
                              _block_diag_halves(w_rgate[l, 1]), _block_diag_halves(w_igate[l, 1])],
                             axis=2)
        wg = wg.astype(BF16)
        lru_n, w_down16, w_out16 = _lru(
            xl, gl, conv_w[l], conv_b[l][None, :], wg, b_rgate[l], b_igate[l],
            lru_lambda[l], out_norm_lru[l][None, :], w_down[l], w_out[l], batch, seq)

        h = _ffn(h, attn_nt, lru_n, w_out16, norm_ffn[l][None, :], w_gate16, w_up16, w_down16)
    return h.reshape(batch, seq, D_MODEL)
```

```python
import jax
import jax.numpy as jnp
import numpy as np
from jax import lax
from jax.experimental import pallas as pl
from jax.experimental.pallas import tpu as pltpu

F32 = jnp.float32
BF16 = jnp.bfloat16

D_MODEL = 1024
GRID_W = 64
HEAD_DIM = 64
N_Q_HEADS = 8
N_KV_HEADS = 2
ATTN_WIDTH = N_Q_HEADS * HEAD_DIM
KV_WIDTH = N_KV_HEADS * HEAD_DIM
LRU_WIDTH = D_MODEL - ATTN_WIDTH
LRU_BLOCKS = 8
LRU_BLOCK = LRU_WIDTH // LRU_BLOCKS
CONV_WIDTH = 4
LRU_C = 8.0
D_IN = ATTN_WIDTH + 2 * KV_WIDTH + 2 * LRU_WIDTH
ROPE_THETA = 10000.0
EPS = 1e-6

LANES = 128
SUBLANES = 8
MXU_DIM = 256
VMEM_LIMIT = 56 * 1024 * 1024

TM_PROJ = 1024
TM_FFN = 1024
TQ = 512
SHIFT_BOUND_MAX = 40.0
BOUND_MARGIN = 1.02
FF_CHUNK = MXU_DIM
GATE_ROWS = 256
OUT_ROWS = 256
N_SEG = SUBLANES
SEG_SKEW = 1
LOG2E = 1.4426950408889634
V_ROWS = HEAD_DIM + 16


def _params(n_grid_axes):
    return pltpu.CompilerParams(dimension_semantics=("arbitrary",) * n_grid_axes,
                                vmem_limit_bytes=VMEM_LIMIT)


def _rms(x, gain):
    return x * lax.rsqrt(jnp.mean(x * x, axis=-1, keepdims=True) + EPS) * gain


def _inproj_kernel(x_ref, g_ref, w32_ref, cos_ref, sa_ref, sb_ref, qg_ref, kg_ref, bd_ref,
                   wgate32_ref, wup32_ref,
                   q_ref, k_ref, vt_ref, xl_ref, gl_ref, wgate_ref, wup_ref, w_ref):
    @pl.when(pl.program_id(0) == 0)
    def _():
        w_ref[...] = w32_ref[...].astype(BF16)

    wgate_ref[...] = wgate32_ref[...].astype(BF16)
    wup_ref[...] = wup32_ref[...].astype(BF16)

    u = _rms(x_ref[...], g_ref[...]).astype(BF16)
    cos = cos_ref[...]
    sin_a = sa_ref[...]
    sin_b = sb_ref[...]
    bd = bd_ref[...]
    lane = lax.broadcasted_iota(jnp.int32, (1, LANES), 1)
    low = lane < HEAD_DIM

    def norm_rope(z, gain):
        sq = z * z
        hi = sq.astype(BF16)
        lo = (sq - hi.astype(F32)).astype(BF16)
        ss = jnp.dot(jnp.concatenate([hi, lo], axis=-1), bd, preferred_element_type=F32)
        y = z * lax.rsqrt(ss * (1.0 / HEAD_DIM) + EPS) * gain
        return (y * cos + pltpu.roll(y, LANES - 16, 1) * sin_a
                + pltpu.roll(y, 16, 1) * sin_b)

    qg = qg_ref[...]
    zq = jnp.dot(u, w_ref[:, 0:ATTN_WIDTH], preferred_element_type=F32)
    for c in range(ATTN_WIDTH // LANES):
        q_ref[:, c * LANES:(c + 1) * LANES] = norm_rope(
            zq[:, c * LANES:(c + 1) * LANES], qg).astype(BF16)

    o = ATTN_WIDTH
    zkv = jnp.dot(u, w_ref[:, o:o + 2 * KV_WIDTH], preferred_element_type=F32)
    k2 = norm_rope(zkv[:, 0:KV_WIDTH], kg_ref[...])
    v2 = zkv[:, KV_WIDTH:2 * KV_WIDTH]
    o += 2 * KV_WIDTH

    a0 = jnp.where(low, k2, 0.0)
    b1 = jnp.where(low, 0.0, k2)
    k_ref[...] = jnp.concatenate(
        [a0, pltpu.roll(a0, HEAD_DIM, 1), pltpu.roll(b1, HEAD_DIM, 1), b1], axis=-1).astype(BF16)

    vt = v2.T.astype(BF16)
    ones = jnp.ones((V_ROWS - HEAD_DIM, vt.shape[1]), BF16)
    for g in range(N_KV_HEADS):
        vt_ref[0, g, 0:HEAD_DIM, :] = vt[g * HEAD_DIM:(g + 1) * HEAD_DIM, :]
        vt_ref[0, g, HEAD_DIM:V_ROWS, :] = ones
    gl_ref[...] = jax.nn.gelu(
        jnp.dot(u, w_ref[:, o + LRU_WIDTH:o + 2 * LRU_WIDTH], preferred_element_type=F32))
    xl_ref[...] = jnp.dot(u, w_ref[:, o:o + LRU_WIDTH], preferred_element_type=F32)


def _inproj(x2, norm_g, w_in, cos2, sin_a, sin_b, qg, kg, bd, w_gate, w_up, batch, seq):
    t = x2.shape[0]
    tm = TM_PROJ
    steps = t // tm
    per_seq = seq // tm
    d_ff = w_gate.shape[1]
    slab = D_MODEL // steps
    row = lambda i: (i, 0)
    const = lambda i: (0, 0)
    pos = lambda i: (i % per_seq, 0)
    return pl.pallas_call(
        _inproj_kernel,
        grid=(steps,),
        in_specs=[
            pl.BlockSpec((tm, D_MODEL), row),
            pl.BlockSpec((1, D_MODEL), const),
            pl.BlockSpec((D_MODEL, D_IN), const, pipeline_mode=pl.Buffered(1)),
            pl.BlockSpec((tm, LANES), pos),
            pl.BlockSpec((tm, LANES), pos),
            pl.BlockSpec((tm, LANES), pos),
            pl.BlockSpec((1, LANES), const),
            pl.BlockSpec((1, LANES), const),
            pl.BlockSpec((2 * LANES, LANES), const),
            pl.BlockSpec((slab, d_ff), row),
            pl.BlockSpec((slab, d_ff), row),
        ],
        out_specs=[
            pl.BlockSpec((tm, ATTN_WIDTH), row),
            pl.BlockSpec((tm, 4 * LANES), row),
            pl.BlockSpec((1, N_KV_HEADS, V_ROWS, tm), lambda i: (i // per_seq, 0, 0, i % per_seq)),
            pl.BlockSpec((tm, LRU_WIDTH), row),
            pl.BlockSpec((tm, LRU_WIDTH), row),
            pl.BlockSpec((slab, d_ff), row),
            pl.BlockSpec((slab, d_ff), row),
        ],
        out_shape=[
            jax.ShapeDtypeStruct((t, ATTN_WIDTH), BF16),
            jax.ShapeDtypeStruct((t, 4 * LANES), BF16),
            jax.ShapeDtypeStruct((batch, N_KV_HEADS, V_ROWS, seq), BF16),
            jax.ShapeDtypeStruct((t, LRU_WIDTH), F32),
            jax.ShapeDtypeStruct((t, LRU_WIDTH), F32),
            jax.ShapeDtypeStruct((D_MODEL, d_ff), BF16),
            jax.ShapeDtypeStruct((D_MODEL, d_ff), BF16),
        ],
        scratch_shapes=[pltpu.VMEM((D_MODEL, D_IN), BF16)],
        compiler_params=_params(1),
        name="inproj",
    )(x2, norm_g, w_in, cos2, sin_a, sin_b, qg, kg, bd, w_gate, w_up)


def _attn_kernel(bound_ref, q_ref, k_ref, vt_ref, g_ref, o_ref, ot_ref, s0_ref, s1_ref):
    nt = (((1,), (1,)), ((), ()))
    group = N_Q_HEADS // N_KV_HEADS
    s_slots = (s0_ref, s1_ref)

    def scores(h):
        col = (2 * (h // group) + h % 2) * LANES
        return lax.dot_general(k_ref[:, col:col + LANES], q_ref[:, (h // 2) * LANES:(h // 2 + 1) * LANES],
                               nt, preferred_element_type=F32)

    def weighted_values(h, p):
        ot = jnp.dot(vt_ref[0, h // group], p, preferred_element_type=F32)
        inv = 1.0 / ot[HEAD_DIM:HEAD_DIM + 1, :]
        ot_ref[h * HEAD_DIM:(h + 1) * HEAD_DIM, :] = ot[0:HEAD_DIM, :] * inv

    bound = bound_ref[0, 0]

    @pl.when(bound <= SHIFT_BOUND_MAX)
    def _():
        for h in range(N_Q_HEADS):
            weighted_values(h, jnp.exp2(scores(h) - bound).astype(BF16))

    @pl.when(bound > SHIFT_BOUND_MAX)
    def _():
        def stage_scores(h):
            s = scores(h)
            s_slots[h % 2][...] = s
            return jnp.max(s, axis=0, keepdims=True)

        m = stage_scores(0)
        for h in range(N_Q_HEADS):
            m_next = stage_scores(h + 1) if h + 1 < N_Q_HEADS else None
            weighted_values(h, jnp.exp2(s_slots[h % 2][...] - m).astype(BF16))
            m = m_next

    ot = ot_ref[...]
    scale = lax.rsqrt(jnp.mean(ot * ot, axis=0, keepdims=True) + EPS)
    for c in range(ot.shape[1] // LANES):
        cols = slice(c * LANES, (c + 1) * LANES)
        o_ref[0, :, cols] = (ot[:, cols] * scale[:, cols] * g_ref[...]).astype(BF16)


def _attention(bound, q, k, vt, gain, batch, seq):
    t = q.shape[0]
    nq = seq // TQ
    return pl.pallas_call(
        _attn_kernel,
        grid=(batch, nq),
        in_specs=[
            pl.BlockSpec(memory_space=pltpu.SMEM),
            pl.BlockSpec((TQ, ATTN_WIDTH), lambda b, i: (b * nq + i, 0)),
            pl.BlockSpec((seq, 4 * LANES), lambda b, i: (b, 0)),
            pl.BlockSpec((1, N_KV_HEADS, V_ROWS, seq), lambda b, i: (b, 0, 0, 0)),
            pl.BlockSpec((ATTN_WIDTH, LANES), lambda b, i: (0, 0)),
        ],
        out_specs=pl.BlockSpec((1, ATTN_WIDTH, TQ), lambda b, i: (b, 0, i)),
        out_shape=jax.ShapeDtypeStruct((batch, ATTN_WIDTH, seq), BF16),
        scratch_shapes=[pltpu.VMEM((ATTN_WIDTH, TQ), F32), pltpu.VMEM((seq, TQ), F32),
                        pltpu.VMEM((seq, TQ), F32)],
        compiler_params=_params(2),
        name="attention",
    )(bound, q, k, vt, gain)


def _lru_kernel(xl_ref, gl_ref, cw_ref, cb_ref, wg_ref, br_ref, bi_ref, lam_ref, g_ref,
                wdown32_ref, wout32_ref, o_ref, wdown_ref, wout_ref,
                xcol, a_f, b_f, a_b, b_b, cf_ref, cbk_ref):
    wdown_ref[...] = wdown32_ref[...].astype(BF16)
    wout_ref[...] = wout32_ref[...].astype(BF16)

    seq = xl_ref.shape[0]
    seg = seq // N_SEG
    sst = seg + SEG_SKEW
    n_col = LRU_WIDTH // LANES
    half_cols = n_col // 2
    half_w = LRU_WIDTH // 2
    pad = SUBLANES

    zeros_pad = jnp.zeros((pad, LANES), F32)
    for c in range(n_col):
        xcol[c, 0:pad, :] = zeros_pad
        xcol[c, pad + seq:pad + seq + pad, :] = zeros_pad
        xcol[c, pad:pad + seq, :] = xl_ref[:, c * LANES:(c + 1) * LANES]

    lam = lam_ref[...]
    c_sp = LRU_C * (jnp.maximum(-lam, 0.0) + jnp.log1p(jnp.exp(-jnp.abs(lam))))
    cw = cw_ref[...]
    cbias = cb_ref[...]
    br = br_ref[...]
    bi = bi_ref[...]
    a_scr = (a_f, a_b)
    b_scr = (b_f, b_b)

    def gate_chunk(ci, _):
        r0 = pl.multiple_of(ci * GATE_ROWS, GATE_ROWS)
        dst = pl.ds(r0 + (r0 // seg) * SEG_SKEW, GATE_ROWS)
        xc = []
        for c in range(n_col):
            cs = slice(c * LANES, (c + 1) * LANES)
            acc = cbias[:, cs]
            for k in range(CONV_WIDTH):
                acc = acc + xcol[c, pl.ds(r0 + pad - 1 + k, GATE_ROWS), :] * cw[k:k + 1, cs]
            xc.append(acc)
        for h in range(2):
            lhs = jnp.concatenate(xc[h * half_cols:(h + 1) * half_cols], axis=-1).astype(BF16)
            z = jnp.dot(lhs, wg_ref[h], preferred_element_type=F32)
            for cc in range(half_cols):
                c = h * half_cols + cc
                cs = slice(c * LANES, (c + 1) * LANES)
                for d in range(2):
                    zo = 2 * d * half_w + cc * LANES
                    r = jax.nn.sigmoid(z[:, zo:zo + LANES] + br[d:d + 1, cs])
                    i = jax.nn.sigmoid(z[:, zo + half_w:zo + half_w + LANES] + bi[d:d + 1, cs])
                    w = c_sp[d:d + 1, cs] * r
                    a = jnp.exp2(w * (-LOG2E))
                    v = jnp.tanh(w) * (a * a + 1.0)
                    mult = jnp.where(v > 0.0, v * lax.rsqrt(v), 0.0)
                    a_scr[d][c, dst, :] = a
                    b_scr[d][c, dst, :] = mult * (i * xc[c])
        return 0

    lax.fori_loop(0, seq // GATE_ROWS, gate_chunk, 0, unroll=True)

    def scan_step(t, carry):
        hf, pf, hb, pb = carry
        idx_f = pl.ds(t, N_SEG, stride=sst)
        idx_b = pl.ds(seg - 1 - t, N_SEG, stride=sst)
        nhf, npf, nhb, npb = [], [], [], []
        for c in range(n_col):
            af = a_f[c, idx_f, :]
            h = af * hf[c] + b_f[c, idx_f, :]
            p = af * pf[c]
            b_f[c, idx_f, :] = h
            a_f[c, idx_f, :] = p
            nhf.append(h)
            npf.append(p)
            ab = a_b[c, idx_b, :]
            h = ab * hb[c] + b_b[c, idx_b, :]
            p = ab * pb[c]
            b_b[c, idx_b, :] = h
            a_b[c, idx_b, :] = p
            nhb.append(h)
            npb.append(p)
        return tuple(nhf), tuple(npf), tuple(nhb), tuple(npb)

    zero = tuple(jnp.zeros((N_SEG, LANES), F32) for _ in range(n_col))
    one = tuple(jnp.ones((N_SEG, LANES), F32) for _ in range(n_col))
    hf, pf, hb, pb = lax.fori_loop(0, seg, scan_step, (zero, one, zero, one), unroll=4)

    for c in range(n_col):
        cin = jnp.zeros((1, LANES), F32)
        for j in range(N_SEG):
            cf_ref[c, j:j + 1, :] = cin
            cin = hf[c][j:j + 1, :] + pf[c][j:j + 1, :] * cin
        cin = jnp.zeros((1, LANES), F32)
        for j in range(N_SEG - 1, -1, -1):
            cbk_ref[c, j:j + 1, :] = cin
            cin = hb[c][j:j + 1, :] + pb[c][j:j + 1, :] * cin

    gain = g_ref[...]

    def out_chunk(ci, _):
        r0 = pl.multiple_of(ci * OUT_ROWS, OUT_ROWS)
        rows = pl.ds(r0, OUT_ROWS)
        j = r0 // seg
        src = pl.ds(r0 + j * SEG_SKEW, OUT_ROWS)
        y = jnp.concatenate(
            [b_f[c, src, :] + a_f[c, src, :] * cf_ref[c, pl.ds(j, 1), :]
             + b_b[c, src, :] + a_b[c, src, :] * cbk_ref[c, pl.ds(j, 1), :]
             for c in range(n_col)], axis=-1)
        out = y * gl_ref[rows, :]
        o_ref[rows, :] = _rms(out, gain).astype(BF16)
        return 0

    lax.fori_loop(0, seq // OUT_ROWS, out_chunk, 0, unroll=True)


def _lru(xl, gl, conv_w, conv_b, wg, b_r, b_i, lam, gain, w_down, w_out, batch, seq):
    t = xl.shape[0]
    d_ff = w_down.shape[0]
    row = lambda b: (b, 0)
    const2 = lambda b: (0, 0)
    return pl.pallas_call(
        _lru_kernel,
        grid=(batch,),
        in_specs=[
            pl.BlockSpec((seq, LRU_WIDTH), row),
            pl.BlockSpec((seq, LRU_WIDTH), row),
            pl.BlockSpec((CONV_WIDTH, LRU_WIDTH), const2),
            pl.BlockSpec((1, LRU_WIDTH), const2),
            pl.BlockSpec((2, LRU_WIDTH // 2, 2 * LRU_WIDTH), lambda b: (0, 0, 0)),
            pl.BlockSpec((2, LRU_WIDTH), const2),
            pl.BlockSpec((2, LRU_WIDTH), const2),
            pl.BlockSpec((2, LRU_WIDTH), const2),
            pl.BlockSpec((1, LRU_WIDTH), const2),
            pl.BlockSpec((d_ff // batch, D_MODEL), row),
            pl.BlockSpec((D_MODEL // batch, D_MODEL), row),
        ],
        out_specs=[
            pl.BlockSpec((seq, LRU_WIDTH), row),
            pl.BlockSpec((d_ff // batch, D_MODEL), row),
            pl.BlockSpec((D_MODEL // batch, D_MODEL), row),
        ],
        out_shape=[
            jax.ShapeDtypeStruct((t, LRU_WIDTH), BF16),
            jax.ShapeDtypeStruct((d_ff, D_MODEL), BF16),
            jax.ShapeDtypeStruct((D_MODEL, D_MODEL), BF16),
        ],
        scratch_shapes=[
            pltpu.VMEM((LRU_WIDTH // LANES, seq + 2 * SUBLANES, LANES), F32),
            pltpu.VMEM((LRU_WIDTH // LANES, seq + N_SEG * SEG_SKEW, LANES), F32),
            pltpu.VMEM((LRU_WIDTH // LANES, seq + N_SEG * SEG_SKEW, LANES), F32),
            pltpu.VMEM((LRU_WIDTH // LANES, seq + N_SEG * SEG_SKEW, LANES), F32),
            pltpu.VMEM((LRU_WIDTH // LANES, seq + N_SEG * SEG_SKEW, LANES), F32),
            pltpu.VMEM((LRU_WIDTH // LANES, N_SEG, LANES), F32),
            pltpu.VMEM((LRU_WIDTH // LANES, N_SEG, LANES), F32),
        ],
        compiler_params=_params(1),
        name="rglru",
    )(xl, gl, conv_w, conv_b, wg, b_r, b_i, lam, gain, w_down, w_out)


def _ffn_kernel(x_ref, at_ref, lr_ref, wo_ref, g_ref, wg_ref, wu_ref, wd_ref, o_ref, ff_ref):
    tn = (((0,), (0,)), ((), ()))
    h1 = (x_ref[...]
          + lax.dot_general(at_ref[0], wo_ref[0:ATTN_WIDTH, :], tn, preferred_element_type=F32)
          + jnp.dot(lr_ref[...], wo_ref[ATTN_WIDTH:, :], preferred_element_type=F32))
    u = _rms(h1, g_ref[...]).astype(BF16)
    o_ref[...] = h1

    for c in range(wg_ref.shape[1] // FF_CHUNK):
        cols = slice(c * FF_CHUNK, (c + 1) * FF_CHUNK)
        gate = jnp.dot(u, wg_ref[:, cols], preferred_element_type=F32)
        up = jnp.dot(u, wu_ref[:, cols], preferred_element_type=F32)
        ff_ref[:, cols] = (jax.nn.silu(gate) * up).astype(BF16)

    o_ref[...] += jnp.dot(ff_ref[...], wd_ref[...], preferred_element_type=F32)


def _ffn(x2, attn_nt, lru_n, w_out, norm_g, wg, wu, wd):
    t = x2.shape[0]
    tm = TM_FFN
    d_ff = wg.shape[1]
    per_seq = attn_nt.shape[2] // tm
    row = lambda i: (i, 0)
    const2 = lambda i: (0, 0)
    once = pl.Buffered(1)
    return pl.pallas_call(
        _ffn_kernel,
        grid=(t // tm,),
        in_specs=[
            pl.BlockSpec((tm, D_MODEL), row),
            pl.BlockSpec((1, ATTN_WIDTH, tm), lambda i: (i // per_seq, 0, i % per_seq)),
            pl.BlockSpec((tm, LRU_WIDTH), row),
            pl.BlockSpec((D_MODEL, D_MODEL), const2, pipeline_mode=once),
            pl.BlockSpec((1, D_MODEL), const2),
            pl.BlockSpec((D_MODEL, d_ff), const2, pipeline_mode=once),
            pl.BlockSpec((D_MODEL, d_ff), const2, pipeline_mode=once),
            pl.BlockSpec((d_ff, D_MODEL), const2, pipeline_mode=once),
        ],
        out_specs=pl.BlockSpec((tm, D_MODEL), row),
        out_shape=jax.ShapeDtypeStruct((t, D_MODEL), F32),
        scratch_shapes=[pltpu.VMEM((tm, d_ff), BF16)],
        compiler_params=_params(1),
        name="outproj_ffn",
    )(x2, attn_nt, lru_n, w_out, norm_g, wg, wu, wd)


def _rope_tables(seq):
    f32 = np.float32
    rows = seq // GRID_W
    row = np.repeat(np.arange(rows, dtype=f32), GRID_W)
    col = np.tile(np.arange(GRID_W, dtype=f32), rows)
    axis_dim = HEAD_DIM // 2
    inv = np.power(f32(ROPE_THETA), -np.arange(0, axis_dim, 2, dtype=f32) / f32(axis_dim)).astype(f32)
    ang_r = row[:, None] * inv[None, :]
    ang_c = col[:, None] * inv[None, :]
    ang = np.concatenate([ang_r, ang_r, ang_c, ang_c], axis=-1)
    ang = np.concatenate([ang, ang], axis=-1)
    first = (np.arange(LANES) % axis_dim) < (axis_dim // 2)
    cos = np.cos(ang).astype(f32)
    sin = np.sin(ang).astype(f32)
    sin_a = np.where(first[None, :], -sin, f32(0.0)).astype(f32)
    sin_b = np.where(first[None, :], f32(0.0), sin).astype(f32)
    return jnp.asarray(cos), jnp.asarray(sin_a), jnp.asarray(sin_b)


def _block_diag_halves(w):
    per = (LRU_WIDTH // 2) // LRU_BLOCK
    blocks = w.reshape(2, per, LRU_BLOCK, LRU_BLOCK)
    eye = jnp.eye(per, dtype=w.dtype)
    dense = blocks[:, :, :, None, :] * eye[None, :, None, :, None]
    return dense.reshape(2, LRU_WIDTH // 2, LRU_WIDTH // 2)


def kernel(x, norm_mix, w_in, q_norm, k_norm, conv_w, conv_b, w_rgate, b_rgate, w_igate, b_igate, lru_lambda, out_norm_attn, out_norm_lru, w_out, norm_ffn, w_gate, w_up, w_down):
    batch, seq, _ = x.shape
    depth = norm_mix.shape[0]
    cos2, sin_a, sin_b = _rope_tables(seq)
    head_id = jnp.arange(LANES) // HEAD_DIM
    bd = (head_id[:, None] == head_id[None, :]).astype(BF16)
    bd = jnp.concatenate([bd, bd], axis=0)

    h = x.reshape(batch * seq, D_MODEL)
    for l in range(depth):
        qg = jnp.tile(q_norm[l] * (HEAD_DIM ** -0.5 * LOG2E), 2)[None, :]
        kg = jnp.tile(k_norm[l], 2)[None, :]
        q, k, vt, xl, gl, w_gate16, w_up16 = _inproj(
            h, norm_mix[l][None, :], w_in[l], cos2, sin_a, sin_b, qg, kg, bd,
            w_gate[l], w_up[l], batch, seq)
        bound = (BOUND_MARGIN * HEAD_DIM * (HEAD_DIM ** -0.5 * LOG2E)
                 * jnp.max(jnp.abs(q_norm[l])) * jnp.max(jnp.abs(k_norm[l]))).reshape(1, 1)
        attn_gain = jnp.broadcast_to(out_norm_attn[l][:, None], (ATTN_WIDTH, LANES))
        attn_nt = _attention(bound, q, k, vt, attn_gain, batch, seq)

        wg = jnp.concatenate([_block_diag_halves(w_rgate[l, 0]), _block_diag_halves(w_igate[l, 0]),
                              _block_diag_halves(w_rgate[l, 1]), _block_diag_halves(w_igate[l, 1])],
                             axis=2)
        wg = wg.astype(BF16)
        lru_n, w_down16, w_out16 = _lru(
            xl, gl, conv_w[l], conv_b[l][None, :], wg, b_rgate[l], b_igate[l],
            lru_lambda[l], out_norm_lru[l][None, :], w_down[l], w_out[l], batch, seq)

        h = _ffn(h, attn_nt, lru_n, w_out16, norm_ffn[l][None, :], w_gate16, w_up16, w_down16)
    return h.reshape(batch, seq, D_MODEL)
```

```python
import jax
import jax.numpy as jnp
import numpy as np
from jax import lax
from jax.experimental import pallas as pl
from jax.experimental.pallas import tpu as pltpu

F32 = jnp.float32
BF16 = jnp.bfloat16

D_MODEL = 1024
GRID_W = 64
HEAD_DIM = 64
N_Q_HEADS = 8
N_KV_HEADS = 2
ATTN_WIDTH = N_Q_HEADS * HEAD_DIM
KV_WIDTH = N_KV_HEADS * HEAD_DIM
LRU_WIDTH = D_MODEL - ATTN_WIDTH
LRU_BLOCKS = 8
LRU_BLOCK = LRU_WIDTH // LRU_BLOCKS
CONV_WIDTH = 4
LRU_C = 8.0
D_IN = ATTN_WIDTH + 2 * KV_WIDTH + 2 * LRU_WIDTH
ROPE_THETA = 10000.0
EPS = 1e-6

LANES = 128
SUBLANES = 8
MXU_DIM = 256
VMEM_LIMIT = 56 * 1024 * 1024

TM_PROJ = 1024
TM_FFN = 1024
TQ = 512
SHIFT_BOUND_MAX = 40.0
BOUND_MARGIN = 1.02
FF_CHUNK = MXU_DIM
GATE_ROWS = 256
OUT_ROWS = 256
N_SEG = SUBLANES
SEG_SKEW = 1
LOG2E = 1.4426950408889634
V_ROWS = HEAD_DIM + 16


def _params(n_grid_axes):
    return pltpu.CompilerParams(dimension_semantics=("arbitrary",) * n_grid_axes,
                                vmem_limit_bytes=VMEM_LIMIT)


def _rms(x, gain):
    return x * lax.rsqrt(jnp.mean(x * x, axis=-1, keepdims=True) + EPS) * gain


def _inproj_kernel(x_ref, g_ref, w32_ref, cos_ref, sa_ref, sb_ref, qg_ref, kg_ref, bd_ref,
                   wgate32_ref, wup32_ref,
                   q_ref, k_ref, vt_ref, xl_ref, gl_ref, wgate_ref, wup_ref, w_ref):
    @pl.when(pl.program_id(0) == 0)
    def _():
        w_ref[...] = w32_ref[...].astype(BF16)

    wgate_ref[...] = wgate32_ref[...].astype(BF16)
    wup_ref[...] = wup32_ref[...].astype(BF16)

    u = _rms(x_ref[...], g_ref[...]).astype(BF16)
    cos = cos_ref[...]
    sin_a = sa_ref[...]
    sin_b = sb_ref[...]
    bd = bd_ref[...]
    lane = lax.broadcasted_iota(jnp.int32, (1, LANES), 1)
    low = lane < HEAD_DIM

    def norm_rope(z, gain):
        sq = z * z
        hi = sq.astype(BF16)
        lo = (sq - hi.astype(F32)).astype(BF16)
        ss = jnp.dot(jnp.concatenate([hi, lo], axis=-1), bd, preferred_element_type=F32)
        y = z * lax.rsqrt(ss * (1.0 / HEAD_DIM) + EPS) * gain
        return (y * cos + pltpu.roll(y, LANES - 16, 1) * sin_a
                + pltpu.roll(y, 16, 1) * sin_b)

    qg = qg_ref[...]
    zq = jnp.dot(u, w_ref[:, 0:ATTN_WIDTH], preferred_element_type=F32)
    for c in range(ATTN_WIDTH // LANES):
        q_ref[:, c * LANES:(c + 1) * LANES] = norm_rope(
            zq[:, c * LANES:(c + 1) * LANES], qg).astype(BF16)

    o = ATTN_WIDTH
    zkv = jnp.dot(u, w_ref[:, o:o + 2 * KV_WIDTH], preferred_element_type=F32)
    k2 = norm_rope(zkv[:, 0:KV_WIDTH], kg_ref[...])
    v2 = zkv[:, KV_WIDTH:2 * KV_WIDTH]
    o += 2 * KV_WIDTH

    a0 = jnp.where(low, k2, 0.0)
    b1 = jnp.where(low, 0.0, k2)
    k_ref[...] = jnp.concatenate(
        [a0, pltpu.roll(a0, HEAD_DIM, 1), pltpu.roll(b1, HEAD_DIM, 1), b1], axis=-1).astype(BF16)

    vt = v2.T.astype(BF16)
    ones = jnp.ones((V_ROWS - HEAD_DIM, vt.shape[1]), BF16)
    for g in range(N_KV_HEADS):
        vt_ref[0, g, 0:HEAD_DIM, :] = vt[g * HEAD_DIM:(g + 1) * HEAD_DIM, :]
        vt_ref[0, g, HEAD_DIM:V_ROWS, :] = ones
    gl_ref[...] = jax.nn.gelu(
        jnp.dot(u, w_ref[:, o + LRU_WIDTH:o + 2 * LRU_WIDTH], preferred_element_type=F32))
    xl_ref[...] = jnp.dot(u, w_ref[:, o:o + LRU_WIDTH], preferred_element_type=F32)


def _inproj(x2, norm_g, w_in, cos2, sin_a, sin_b, qg, kg, bd, w_gate, w_up, batch, seq):
    t = x2.shape[0]
    tm = TM_PROJ
    steps = t // tm
    per_seq = seq // tm
    d_ff = w_gate.shape[1]
    slab = D_MODEL // steps
    row = lambda i: (i, 0)
    const = lambda i: (0, 0)
    pos = lambda i: (i % per_seq, 0)
    return pl.pallas_call(
        _inproj_kernel,
        grid=(steps,),
        in_specs=[
            pl.BlockSpec((tm, D_MODEL), row),
            pl.BlockSpec((1, D_MODEL), const),
            pl.BlockSpec((D_MODEL, D_IN), const, pipeline_mode=pl.Buffered(1)),
            pl.BlockSpec((tm, LANES), pos),
            pl.BlockSpec((tm, LANES), pos),
            pl.BlockSpec((tm, LANES), pos),
            pl.BlockSpec((1, LANES), const),
            pl.BlockSpec((1, LANES), const),
            pl.BlockSpec((2 * LANES, LANES), const),
            pl.BlockSpec((slab, d_ff), row),
            pl.BlockSpec((slab, d_ff), row),
        ],
        out_specs=[
            pl.BlockSpec((tm, ATTN_WIDTH), row),
            pl.BlockSpec((tm, 4 * LANES), row),
            pl.BlockSpec((1, N_KV_HEADS, V_ROWS, tm), lambda i: (i // per_seq, 0, 0, i % per_seq)),
            pl.BlockSpec((tm, LRU_WIDTH), row),
            pl.BlockSpec((tm, LRU_WIDTH), row),
            pl.BlockSpec((slab, d_ff), row),
            pl.BlockSpec((slab, d_ff), row),
        ],
        out_shape=[
            jax.ShapeDtypeStruct((t, ATTN_WIDTH), BF16),
            jax.ShapeDtypeStruct((t, 4 * LANES), BF16),
            jax.ShapeDtypeStruct((batch, N_KV_HEADS, V_ROWS, seq), BF16),
            jax.ShapeDtypeStruct((t, LRU_WIDTH), F32),
            jax.ShapeDtypeStruct((t, LRU_WIDTH), F32),
            jax.ShapeDtypeStruct((D_MODEL, d_ff), BF16),
            jax.ShapeDtypeStruct((D_MODEL, d_ff), BF16),
        ],
        scratch_shapes=[pltpu.VMEM((D_MODEL, D_IN), BF16)],
        compiler_params=_params(1),
        name="inproj",
    )(x2, norm_g, w_in, cos2, sin_a, sin_b, qg, kg, bd, w_gate, w_up)


def _attn_kernel(bound_ref, q_ref, k_ref, vt_ref, g_ref, o_ref, ot_ref, s0_ref, s1_ref):
    nt = (((1,), (1,)), ((), ()))
    group = N_Q_HEADS // N_KV_HEADS
    s_slots = (s0_ref, s1_ref)

    def scores(h):
        col = (2 * (h // group) + h % 2) * LANES
        return lax.dot_general(k_ref[:, col:col + LANES], q_ref[:, (h // 2) * LANES:(h // 2 + 1) * LANES],
                               nt, preferred_element_type=F32)

    def weighted_values(h, p):
        ot = jnp.dot(vt_ref[0, h // group], p, preferred_element_type=F32)
        inv = 1.0 / ot[HEAD_DIM:HEAD_DIM + 1, :]
        ot_ref[h * HEAD_DIM:(h + 1) * HEAD_DIM, :] = ot[0:HEAD_DIM, :] * inv

    bound = bound_ref[0, 0]

    @pl.when(bound <= SHIFT_BOUND_MAX)
    def _():
        for h in range(N_Q_HEADS):
            weighted_values(h, jnp.exp2(scores(h) - bound).astype(BF16))

    @pl.when(jnp.logical_not(bound <= SHIFT_BOUND_MAX))
    def _():
        def stage_scores(h):
            s = scores(h)
            s_slots[h % 2][...] = s
            return jnp.max(s, axis=0, keepdims=True)

        m = stage_scores(0)
        for h in range(N_Q_HEADS):
            m_next = stage_scores(h + 1) if h + 1 < N_Q_HEADS else None
            weighted_values(h, jnp.exp2(s_slots[h % 2][...] - m).astype(BF16))
            m = m_next

    ot = ot_ref[...]
    scale = lax.rsqrt(jnp.mean(ot * ot, axis=0, keepdims=True) + EPS)
    for c in range(ot.shape[1] // LANES):
        cols = slice(c * LANES, (c + 1) * LANES)
        o_ref[0, :, cols] = (ot[:, cols] * scale[:, cols] * g_ref[...]).astype(BF16)


def _attention(bound, q, k, vt, gain, batch, seq):
    t = q.shape[0]
    nq = seq // TQ
    return pl.pallas_call(
        _attn_kernel,
        grid=(batch, nq),
        in_specs=[
            pl.BlockSpec(memory_space=pltpu.SMEM),
            pl.BlockSpec((TQ, ATTN_WIDTH), lambda b, i: (b * nq + i, 0)),
            pl.BlockSpec((seq, 4 * LANES), lambda b, i: (b, 0)),
            pl.BlockSpec((1, N_KV_HEADS, V_ROWS, seq), lambda b, i: (b, 0, 0, 0)),
            pl.BlockSpec((ATTN_WIDTH, LANES), lambda b, i: (0, 0)),
        ],
        out_specs=pl.BlockSpec((1, ATTN_WIDTH, TQ), lambda b, i: (b, 0, i)),
        out_shape=jax.ShapeDtypeStruct((batch, ATTN_WIDTH, seq), BF16),
        scratch_shapes=[pltpu.VMEM((ATTN_WIDTH, TQ), F32), pltpu.VMEM((seq, TQ), F32),
                        pltpu.VMEM((seq, TQ), F32)],
        compiler_params=_params(2),
        name="attention",
    )(bound, q, k, vt, gain)


def _lru_kernel(xl_ref, gl_ref, cw_ref, cb_ref, wg_ref, br_ref, bi_ref, lam_ref, g_ref,
                wdown32_ref, wout32_ref, o_ref, wdown_ref, wout_ref,
                xcol, a_f, b_f, a_b, b_b, cf_ref, cbk_ref):
    wdown_ref[...] = wdown32_ref[...].astype(BF16)
    wout_ref[...] = wout32_ref[...].astype(BF16)

    seq = xl_ref.shape[0]
    seg = seq // N_SEG
    sst = seg + SEG_SKEW
    n_col = LRU_WIDTH // LANES
    half_cols = n_col // 2
    half_w = LRU_WIDTH // 2
    pad = SUBLANES

    zeros_pad = jnp.zeros((pad, LANES), F32)
    for c in range(n_col):
        xcol[c, 0:pad, :] = zeros_pad
        xcol[c, pad + seq:pad + seq + pad, :] = zeros_pad
        xcol[c, pad:pad + seq, :] = xl_ref[:, c * LANES:(c + 1) * LANES]

    lam = lam_ref[...]
    c_sp = LRU_C * (jnp.maximum(-lam, 0.0) + jnp.log1p(jnp.exp(-jnp.abs(lam))))
    cw = cw_ref[...]
    cbias = cb_ref[...]
    br = br_ref[...]
    bi = bi_ref[...]
    a_scr = (a_f, a_b)
    b_scr = (b_f, b_b)

    def gate_chunk(ci, _):
        r0 = pl.multiple_of(ci * GATE_ROWS, GATE_ROWS)
        dst = pl.ds(r0 + (r0 // seg) * SEG_SKEW, GATE_ROWS)
        xc = []
        for c in range(n_col):
            cs = slice(c * LANES, (c + 1) * LANES)
            acc = cbias[:, cs]
            for k in range(CONV_WIDTH):
                acc = acc + xcol[c, pl.ds(r0 + pad - 1 + k, GATE_ROWS), :] * cw[k:k + 1, cs]
            xc.append(acc)
        for h in range(2):
            lhs = jnp.concatenate(xc[h * half_cols:(h + 1) * half_cols], axis=-1).astype(BF16)
            z = jnp.dot(lhs, wg_ref[h], preferred_element_type=F32)
            for cc in range(half_cols):
                c = h * half_cols + cc
                cs = slice(c * LANES, (c + 1) * LANES)
                for d in range(2):
                    zo = 2 * d * half_w + cc * LANES
                    r = jax.nn.sigmoid(z[:, zo:zo + LANES] + br[d:d + 1, cs])
                    i = jax.nn.sigmoid(z[:, zo + half_w:zo + half_w + LANES] + bi[d:d + 1, cs])
                    w = c_sp[d:d + 1, cs] * r
                    a = jnp.exp2(w * (-LOG2E))
                    v = jnp.tanh(w) * (a * a + 1.0)
                    mult = jnp.where(v > 0.0, v * lax.rsqrt(v), 0.0)
                    a_scr[d][c, dst, :] = a
                    b_scr[d][c, dst, :] = mult * (i * xc[c])
        return 0

    lax.fori_loop(0, seq // GATE_ROWS, gate_chunk, 0, unroll=True)

    def scan_step(t, carry):
        hf, pf, hb, pb = carry
        idx_f = pl.ds(t, N_SEG, stride=sst)
        idx_b = pl.ds(seg - 1 - t, N_SEG, stride=sst)
        nhf, npf, nhb, npb = [], [], [], []
        for c in range(n_col):
            af = a_f[c, idx_f, :]
            h = af * hf[c] + b_f[c, idx_f, :]
            p = af * pf[c]
            b_f[c, idx_f, :] = h
            a_f[c, idx_f, :] = p
            nhf.append(h)
            npf.append(p)
            ab = a_b[c, idx_b, :]
            h = ab * hb[c] + b_b[c, idx_b, :]
            p = ab * pb[c]
            b_b[c, idx_b, :] = h
            a_b[c, idx_b, :] = p
            nhb.append(h)
            npb.append(p)
        return tuple(nhf), tuple(npf), tuple(nhb), tuple(npb)

    zero = tuple(jnp.zeros((N_SEG, LANES), F32) for _ in range(n_col))
    one = tuple(jnp.ones((N_SEG, LANES), F32) for _ in range(n_col))
    hf, pf, hb, pb = lax.fori_loop(0, seg, scan_step, (zero, one, zero, one), unroll=4)

    for c in range(n_col):
        cin = jnp.zeros((1, LANES), F32)
        for j in range(N_SEG):
            cf_ref[c, j:j + 1, :] = cin
            cin = hf[c][j:j + 1, :] + pf[c][j:j + 1, :] * cin
        cin = jnp.zeros((1, LANES), F32)
        for j in range(N_SEG - 1, -1, -1):
            cbk_ref[c, j:j + 1, :] = cin
            cin = hb[c][j:j + 1, :] + pb[c][j:j + 1, :] * cin

    gain = g_ref[...]

    def out_chunk(ci, _):
        r0 = pl.multiple_of(ci * OUT_ROWS, OUT_ROWS)
        rows = pl.ds(r0, OUT_ROWS)
        j = r0 // seg
        src = pl.ds(r0 + j * SEG_SKEW, OUT_ROWS)
        y = jnp.concatenate(
            [b_f[c, src, :] + a_f[c, src, :] * cf_ref[c, pl.ds(j, 1), :]
             + b_b[c, src, :] + a_b[c, src, :] * cbk_ref[c, pl.ds(j, 1), :]
             for c in range(n_col)], axis=-1)
        out = y * gl_ref[rows, :]
        o_ref[rows, :] = _rms(out, gain).astype(BF16)
        return 0

    lax.fori_loop(0, seq // OUT_ROWS, out_chunk, 0, unroll=True)


def _lru(xl, gl, conv_w, conv_b, wg, b_r, b_i, lam, gain, w_down, w_out, batch, seq):
    t = xl.shape[0]
    d_ff = w_down.shape[0]
    row = lambda b: (b, 0)
    const2 = lambda b: (0, 0)
    return pl.pallas_call(
        _lru_kernel,
        grid=(batch,),
        in_specs=[
            pl.BlockSpec((seq, LRU_WIDTH), row),
            pl.BlockSpec((seq, LRU_WIDTH), row),
            pl.BlockSpec((CONV_WIDTH, LRU_WIDTH), const2),
            pl.BlockSpec((1, LRU_WIDTH), const2),
            pl.BlockSpec((2, LRU_WIDTH // 2, 2 * LRU_WIDTH), lambda b: (0, 0, 0)),
            pl.BlockSpec((2, LRU_WIDTH), const2),
            pl.BlockSpec((2, LRU_WIDTH), const2),
            pl.BlockSpec((2, LRU_WIDTH), const2),
            pl.BlockSpec((1, LRU_WIDTH), const2),
            pl.BlockSpec((d_ff // batch, D_MODEL), row),
            pl.BlockSpec((D_MODEL // batch, D_MODEL), row),
        ],
        out_specs=[
            pl.BlockSpec((seq, LRU_WIDTH), row),
            pl.BlockSpec((d_ff // batch, D_MODEL), row),
            pl.BlockSpec((D_MODEL // batch, D_MODEL), row),
        ],
        out_shape=[
            jax.ShapeDtypeStruct((t, LRU_WIDTH), BF16),
            jax.ShapeDtypeStruct((d_ff, D_MODEL), BF16),
            jax.ShapeDtypeStruct((D_MODEL, D_MODEL), BF16),
        ],
        scratch_shapes=[
            pltpu.VMEM((LRU_WIDTH // LANES, seq + 2 * SUBLANES, LANES), F32),
            pltpu.VMEM((LRU_WIDTH // LANES, seq + N_SEG * SEG_SKEW, LANES), F32),
            pltpu.VMEM((LRU_WIDTH // LANES, seq + N_SEG * SEG_SKEW, LANES), F32),
            pltpu.VMEM((LRU_WIDTH // LANES, seq + N_SEG * SEG_SKEW, LANES), F32),
            pltpu.VMEM((LRU_WIDTH // LANES, seq + N_SEG * SEG_SKEW, LANES), F32),
            pltpu.VMEM((LRU_WIDTH // LANES, N_SEG, LANES), F32),
            pltpu.VMEM((LRU_WIDTH // LANES, N_SEG, LANES), F32),
        ],
        compiler_params=_params(1),
        name="rglru",
    )(xl, gl, conv_w, conv_b, wg, b_r, b_i, lam, gain, w_down, w_out)


def _ffn_kernel(x_ref, at_ref, lr_ref, wo_ref, g_ref, wg_ref, wu_ref, wd_ref, o_ref, ff_ref):
    tn = (((0,), (0,)), ((), ()))
    h1 = (x_ref[...]
          + lax.dot_general(at_ref[0], wo_ref[0:ATTN_WIDTH, :], tn, preferred_element_type=F32)
          + jnp.dot(lr_ref[...], wo_ref[ATTN_WIDTH:, :], preferred_element_type=F32))
    u = _rms(h1, g_ref[...]).astype(BF16)
    o_ref[...] = h1

    for c in range(wg_ref.shape[1] // FF_CHUNK):
        cols = slice(c * FF_CHUNK, (c + 1) * FF_CHUNK)
        gate = jnp.dot(u, wg_ref[:, cols], preferred_element_type=F32)
        up = jnp.dot(u, wu_ref[:, cols], preferred_element_type=F32)
        ff_ref[:, cols] = (jax.nn.silu(gate) * up).astype(BF16)

    o_ref[...] += jnp.dot(ff_ref[...], wd_ref[...], preferred_element_type=F32)


def _ffn(x2, attn_nt, lru_n, w_out, norm_g, wg, wu, wd):
    t = x2.shape[0]
    tm = TM_FFN
    d_ff = wg.shape[1]
    per_seq = attn_nt.shape[2] // tm
    row = lambda i: (i, 0)
    const2 = lambda i: (0, 0)
    once = pl.Buffered(1)
    return pl.pallas_call(
        _ffn_kernel,
        grid=(t // tm,),
        in_specs=[
            pl.BlockSpec((tm, D_MODEL), row),
            pl.BlockSpec((1, ATTN_WIDTH, tm), lambda i: (i // per_seq, 0, i % per_seq)),
            pl.BlockSpec((tm, LRU_WIDTH), row),
            pl.BlockSpec((D_MODEL, D_MODEL), const2, pipeline_mode=once),
            pl.BlockSpec((1, D_MODEL), const2),
            pl.BlockSpec((D_MODEL, d_ff), const2, pipeline_mode=once),
            pl.BlockSpec((D_MODEL, d_ff), const2, pipeline_mode=once),
            pl.BlockSpec((d_ff, D_MODEL), const2, pipeline_mode=once),
        ],
        out_specs=pl.BlockSpec((tm, D_MODEL), row),
        out_shape=jax.ShapeDtypeStruct((t, D_MODEL), F32),
        scratch_shapes=[pltpu.VMEM((tm, d_ff), BF16)],
        compiler_params=_params(1),
        name="outproj_ffn",
    )(x2, attn_nt, lru_n, w_out, norm_g, wg, wu, wd)


def _rope_tables(seq):
    f32 = np.float32
    rows = seq // GRID_W
    row = np.repeat(np.arange(rows, dtype=f32), GRID_W)
    col = np.tile(np.arange(GRID_W, dtype=f32), rows)
    axis_dim = HEAD_DIM // 2
    inv = np.power(f32(ROPE_THETA), -np.arange(0, axis_dim, 2, dtype=f32) / f32(axis_dim)).astype(f32)
    ang_r = row[:, None] * inv[None, :]
    ang_c = col[:, None] * inv[None, :]
    ang = np.concatenate([ang_r, ang_r, ang_c, ang_c], axis=-1)
    ang = np.concatenate([ang, ang], axis=-1)
    first = (np.arange(LANES) % axis_dim) < (axis_dim // 2)
    cos = np.cos(ang).astype(f32)
    sin = np.sin(ang).astype(f32)
    sin_a = np.where(first[None, :], -sin, f32(0.0)).astype(f32)
    sin_b = np.where(first[None, :], f32(0.0), sin).astype(f32)
    return jnp.asarray(cos), jnp.asarray(sin_a), jnp.asarray(sin_b)


def _block_diag_halves(w):
    per = (LRU_WIDTH // 2) // LRU_BLOCK
    blocks = w.reshape(2, per, LRU_BLOCK, LRU_BLOCK)
    eye = jnp.eye(per, dtype=w.dtype)
    dense = blocks[:, :, :, None, :] * eye[None, :, None, :, None]
    return dense.reshape(2, LRU_WIDTH // 2, LRU_WIDTH // 2)


def kernel(x, norm_mix, w_in, q_norm, k_norm, conv_w, conv_b, w_rgate, b_rgate, w_igate, b_igate, lru_lambda, out_norm_attn, out_norm_lru, w_out, norm_ffn, w_gate, w_up, w_down):
    batch, seq, _ = x.shape
    depth = norm_mix.shape[0]
    cos2, sin_a, sin_b = _rope_tables(seq)
    head_id = jnp.arange(LANES) // HEAD_DIM
    bd = (head_id[:, None] == head_id[None, :]).astype(BF16)
    bd = jnp.concatenate([bd, bd], axis=0)

    h = x.reshape(batch * seq, D_MODEL)
    for l in range(depth):
        qg = jnp.tile(q_norm[l] * (HEAD_DIM ** -0.5 * LOG2E), 2)[None, :]
        kg = jnp.tile(k_norm[l], 2)[None, :]
        q, k, vt, xl, gl, w_gate16, w_up16 = _inproj(
            h, norm_mix[l][None, :], w_in[l], cos2, sin_a, sin_b, qg, kg, bd,
            w_gate[l], w_up[l], batch, seq)
        bound = (BOUND_MARGIN * HEAD_DIM * (HEAD_DIM ** -0.5 * LOG2E)
                 * jnp.max(jnp.abs(q_norm[l])) * jnp.max(jnp.abs(k_norm[l]))).reshape(1, 1)
        attn_gain = jnp.broadcast_to(out_norm_attn[l][:, None], (ATTN_WIDTH, LANES))
        attn_nt = _attention(bound, q, k, vt, attn_gain, batch, seq)

        wg = jnp.concatenate([_block_diag_halves(w_rgate[l, 0]), _block_diag_halves(w_igate[l, 0]),
                              _block_diag_halves(w_rgate[l, 1]), _block_diag_halves(w_igate[l, 1])],
                             axis=2)
        wg = wg.astype(BF16)
        lru_n, w_down16, w_out16 = _lru(
            xl, gl, conv_w[l], conv_b[l][None, :], wg, b_rgate[l], b_igate[l],
            lru_lambda[l], out_norm_lru[l][None, :], w_down[l], w_out[l], batch, seq)

        h = _ffn(h, attn_nt, lru_n, w_out16, norm_ffn[l][None, :], w_gate16, w_up16, w_down16)
    return h.reshape(batch, seq, D_MODEL)
```

```python
import jax
import jax.numpy as jnp
import numpy as np
from jax import lax
from jax.experimental import pallas as pl
from jax.experimental.pallas import tpu as pltpu

F32 = jnp.float32
BF16 = jnp.bfloat16

D_MODEL = 1024
GRID_W = 64
HEAD_DIM = 64
N_Q_HEADS = 8
N_KV_HEADS = 2
ATTN_WIDTH = N_Q_HEADS * HEAD_DIM
KV_WIDTH = N_KV_HEADS * HEAD_DIM
LRU_WIDTH = D_MODEL - ATTN_WIDTH
LRU_BLOCKS = 8
LRU_BLOCK = LRU_WIDTH // LRU_BLOCKS
CONV_WIDTH = 4
LRU_C = 8.0
D_IN = ATTN_WIDTH + 2 * KV_WIDTH + 2 * LRU_WIDTH
ROPE_THETA = 10000.0
EPS = 1e-6

LANES = 128
SUBLANES = 8
MXU_DIM = 256
VMEM_LIMIT = 56 * 1024 * 1024

TM_PROJ = 1024
X_SLOTS = 3
TM_FFN = 1024
TQ = 512
SHIFT_BOUND_MAX = 40.0
BOUND_MARGIN = 1.02
FF_CHUNK = MXU_DIM
GATE_ROWS = 256
OUT_ROWS = 256
N_SEG = SUBLANES
SEG_SKEW = 1
LOG2E = 1.4426950408889634
V_ROWS = HEAD_DIM + 16


def _params(n_grid_axes):
    return pltpu.CompilerParams(dimension_semantics=("arbitrary",) * n_grid_axes,
                                vmem_limit_bytes=VMEM_LIMIT)


def _rms(x, gain):
    return x * lax.rsqrt(jnp.mean(x * x, axis=-1, keepdims=True) + EPS) * gain


def _inproj_kernel(x_hbm, g_ref, w32_ref, cos_ref, sa_ref, sb_ref, qg_ref, kg_ref, bd_ref,
                   wgate32_ref, wup32_ref,
                   q_ref, k_ref, vt_ref, xl_ref, gl_ref, wgate_ref, wup_ref, w_ref, x_ring, x_sem):
    step = pl.program_id(0)
    steps = pl.num_programs(0)
    tm = x_ring.shape[1]

    def x_copy(s):
        slot = s % X_SLOTS
        return pltpu.make_async_copy(x_hbm.at[pl.ds(s * tm, tm), :], x_ring.at[slot], x_sem.at[slot])

    @pl.when(step == 0)
    def _():
        x_copy(0).start()
        x_copy(1).start()

    @pl.when(step + 2 < steps)
    def _():
        x_copy(step + 2).start()

    @pl.when(step == 0)
    def _():
        w_ref[...] = w32_ref[...].astype(BF16)

    wgate_ref[...] = wgate32_ref[...].astype(BF16)
    wup_ref[...] = wup32_ref[...].astype(BF16)

    x_copy(step).wait()
    u = _rms(x_ring[step % X_SLOTS], g_ref[...]).astype(BF16)
    cos = cos_ref[...]
    sin_a = sa_ref[...]
    sin_b = sb_ref[...]
    bd = bd_ref[...]
    lane = lax.broadcasted_iota(jnp.int32, (1, LANES), 1)
    low = lane < HEAD_DIM

    def norm_rope(z, gain):
        sq = z * z
        hi = sq.astype(BF16)
        lo = (sq - hi.astype(F32)).astype(BF16)
        ss = jnp.dot(jnp.concatenate([hi, lo], axis=-1), bd, preferred_element_type=F32)
        y = z * lax.rsqrt(ss * (1.0 / HEAD_DIM) + EPS) * gain
        return (y * cos + pltpu.roll(y, LANES - 16, 1) * sin_a
                + pltpu.roll(y, 16, 1) * sin_b)

    def proj(start, width):
        return jnp.dot(u, w_ref[:, start:start + width], preferred_element_type=F32)

    qg = qg_ref[...]
    zq = proj(0, ATTN_WIDTH)
    for c in range(ATTN_WIDTH // LANES):
        q_ref[:, c * LANES:(c + 1) * LANES] = norm_rope(
            zq[:, c * LANES:(c + 1) * LANES], qg).astype(BF16)

    o = ATTN_WIDTH
    zkv = proj(o, 2 * KV_WIDTH)
    k2 = norm_rope(zkv[:, 0:KV_WIDTH], kg_ref[...])
    v2 = zkv[:, KV_WIDTH:2 * KV_WIDTH]
    o += 2 * KV_WIDTH

    a0 = jnp.where(low, k2, 0.0)
    b1 = jnp.where(low, 0.0, k2)
    k_ref[...] = jnp.concatenate(
        [a0, pltpu.roll(a0, HEAD_DIM, 1), pltpu.roll(b1, HEAD_DIM, 1), b1], axis=-1).astype(BF16)

    vt = v2.T.astype(BF16)
    ones = jnp.ones((V_ROWS - HEAD_DIM, vt.shape[1]), BF16)
    for g in range(N_KV_HEADS):
        vt_ref[0, g, 0:HEAD_DIM, :] = vt[g * HEAD_DIM:(g + 1) * HEAD_DIM, :]
        vt_ref[0, g, HEAD_DIM:V_ROWS, :] = ones
    gl_ref[...] = jax.nn.gelu(proj(o + LRU_WIDTH, LRU_WIDTH))
    xl_ref[...] = proj(o, LRU_WIDTH)


def _inproj(x2, norm_g, w_in, cos2, sin_a, sin_b, qg, kg, bd, w_gate, w_up, batch, seq):
    t = x2.shape[0]
    tm = TM_PROJ
    steps = t // tm
    assert steps >= X_SLOTS - 1, "the x ring primes two tiles at step 0"
    per_seq = seq // tm
    d_ff = w_gate.shape[1]
    slab = D_MODEL // steps
    row = lambda i: (i, 0)
    const = lambda i: (0, 0)
    pos = lambda i: (i % per_seq, 0)
    return pl.pallas_call(
        _inproj_kernel,
        grid=(steps,),
        in_specs=[
            pl.BlockSpec(memory_space=pl.ANY),
            pl.BlockSpec((1, D_MODEL), const),
            pl.BlockSpec((D_MODEL, D_IN), const, pipeline_mode=pl.Buffered(1)),
            pl.BlockSpec((tm, LANES), pos),
            pl.BlockSpec((tm, LANES), pos),
            pl.BlockSpec((tm, LANES), pos),
            pl.BlockSpec((1, LANES), const),
            pl.BlockSpec((1, LANES), const),
            pl.BlockSpec((2 * LANES, LANES), const),
            pl.BlockSpec((slab, d_ff), row),
            pl.BlockSpec((slab, d_ff), row),
        ],
        out_specs=[
            pl.BlockSpec((tm, ATTN_WIDTH), row),
            pl.BlockSpec((tm, 4 * LANES), row),
            pl.BlockSpec((1, N_KV_HEADS, V_ROWS, tm), lambda i: (i // per_seq, 0, 0, i % per_seq)),
            pl.BlockSpec((tm, LRU_WIDTH), row),
            pl.BlockSpec((tm, LRU_WIDTH), row),
            pl.BlockSpec((slab, d_ff), row),
            pl.BlockSpec((slab, d_ff), row),
        ],
        out_shape=[
            jax.ShapeDtypeStruct((t, ATTN_WIDTH), BF16),
            jax.ShapeDtypeStruct((t, 4 * LANES), BF16),
            jax.ShapeDtypeStruct((batch, N_KV_HEADS, V_ROWS, seq), BF16),
            jax.ShapeDtypeStruct((t, LRU_WIDTH), F32),
            jax.ShapeDtypeStruct((t, LRU_WIDTH), F32),
            jax.ShapeDtypeStruct((D_MODEL, d_ff), BF16),
            jax.ShapeDtypeStruct((D_MODEL, d_ff), BF16),
        ],
        scratch_shapes=[pltpu.VMEM((D_MODEL, D_IN), BF16),
                        pltpu.VMEM((X_SLOTS, tm, D_MODEL), F32),
                        pltpu.SemaphoreType.DMA((X_SLOTS,))],
        compiler_params=_params(1),
        name="inproj",
    )(x2, norm_g, w_in, cos2, sin_a, sin_b, qg, kg, bd, w_gate, w_up)


def _attn_kernel(bound_ref, q_ref, k_ref, vt_ref, g_ref, o_ref, ot_ref, s0_ref, s1_ref):
    nt = (((1,), (1,)), ((), ()))
    group = N_Q_HEADS // N_KV_HEADS
    s_slots = (s0_ref, s1_ref)

    def scores(h):
        col = (2 * (h // group) + h % 2) * LANES
        return lax.dot_general(k_ref[:, col:col + LANES], q_ref[:, (h // 2) * LANES:(h // 2 + 1) * LANES],
                               nt, preferred_element_type=F32)

    def weighted_values(h, p):
        ot = jnp.dot(vt_ref[0, h // group], p, preferred_element_type=F32)
        inv = 1.0 / ot[HEAD_DIM:HEAD_DIM + 1, :]
        ot_ref[h * HEAD_DIM:(h + 1) * HEAD_DIM, :] = ot[0:HEAD_DIM, :] * inv

    bound = bound_ref[0, 0]

    @pl.when(bound <= SHIFT_BOUND_MAX)
    def _():
        for h in range(N_Q_HEADS):
            weighted_values(h, jnp.exp2(scores(h)).astype(BF16))

    @pl.when(jnp.logical_not(bound <= SHIFT_BOUND_MAX))
    def _():
        def stage_scores(h):
            s = scores(h)
            s_slots[h % 2][...] = s
            return jnp.max(s, axis=0, keepdims=True)

        m = stage_scores(0)
        for h in range(N_Q_HEADS):
            m_next = stage_scores(h + 1) if h + 1 < N_Q_HEADS else None
            weighted_values(h, jnp.exp2(s_slots[h % 2][...] - m).astype(BF16))
            m = m_next

    ot = ot_ref[...]
    scale = lax.rsqrt(jnp.mean(ot * ot, axis=0, keepdims=True) + EPS)
    for c in range(ot.shape[1] // LANES):
        cols = slice(c * LANES, (c + 1) * LANES)
        o_ref[0, :, cols] = (ot[:, cols] * scale[:, cols] * g_ref[...]).astype(BF16)


def _attention(bound, q, k, vt, gain, batch, seq):
    t = q.shape[0]
    nq = seq // TQ
    return pl.pallas_call(
        _attn_kernel,
        grid=(batch, nq),
        in_specs=[
            pl.BlockSpec(memory_space=pltpu.SMEM),
            pl.BlockSpec((TQ, ATTN_WIDTH), lambda b, i: (b * nq + i, 0)),
            pl.BlockSpec((seq, 4 * LANES), lambda b, i: (b, 0)),
            pl.BlockSpec((1, N_KV_HEADS, V_ROWS, seq), lambda b, i: (b, 0, 0, 0)),
            pl.BlockSpec((ATTN_WIDTH, LANES), lambda b, i: (0, 0)),
        ],
        out_specs=pl.BlockSpec((1, ATTN_WIDTH, TQ), lambda b, i: (b, 0, i)),
        out_shape=jax.ShapeDtypeStruct((batch, ATTN_WIDTH, seq), BF16),
        scratch_shapes=[pltpu.VMEM((ATTN_WIDTH, TQ), F32), pltpu.VMEM((seq, TQ), F32),
                        pltpu.VMEM((seq, TQ), F32)],
        compiler_params=_params(2),
        name="attention",
    )(bound, q, k, vt, gain)


def _lru_kernel(xl_ref, gl_ref, cw_ref, cb_ref, wg_ref, br_ref, bi_ref, lam_ref, g_ref,
                wdown32_ref, wout32_ref, o_ref, wdown_ref, wout_ref,
                xcol, a_f, b_f, a_b, b_b, cf_ref, cbk_ref):
    wdown_ref[...] = wdown32_ref[...].astype(BF16)
    wout_ref[...] = wout32_ref[...].astype(BF16)

    seq = xl_ref.shape[0]
    seg = seq // N_SEG
    sst = seg + SEG_SKEW
    n_col = LRU_WIDTH // LANES
    half_cols = n_col // 2
    half_w = LRU_WIDTH // 2
    pad = SUBLANES

    zeros_pad = jnp.zeros((pad, LANES), F32)
    for c in range(n_col):
        xcol[c, 0:pad, :] = zeros_pad
        xcol[c, pad + seq:pad + seq + pad, :] = zeros_pad
        xcol[c, pad:pad + seq, :] = xl_ref[:, c * LANES:(c + 1) * LANES]

    lam = lam_ref[...]
    c_sp = LRU_C * (jnp.maximum(-lam, 0.0) + jnp.log1p(jnp.exp(-jnp.abs(lam))))
    cw = cw_ref[...]
    cbias = cb_ref[...]
    br = br_ref[...]
    bi = bi_ref[...]
    a_scr = (a_f, a_b)
    b_scr = (b_f, b_b)

    def gate_chunk(ci, _):
        r0 = pl.multiple_of(ci * GATE_ROWS, GATE_ROWS)
        dst = pl.ds(r0 + (r0 // seg) * SEG_SKEW, GATE_ROWS)
        xc = []
        for c in range(n_col):
            cs = slice(c * LANES, (c + 1) * LANES)
            acc = cbias[:, cs]
            for k in range(CONV_WIDTH):
                acc = acc + xcol[c, pl.ds(r0 + pad - 1 + k, GATE_ROWS), :] * cw[k:k + 1, cs]
            xc.append(acc)
        for h in range(2):
            lhs = jnp.concatenate(xc[h * half_cols:(h + 1) * half_cols], axis=-1).astype(BF16)
            z = jnp.dot(lhs, wg_ref[h], preferred_element_type=F32)
            for cc in range(half_cols):
                c = h * half_cols + cc
                cs = slice(c * LANES, (c + 1) * LANES)
                for d in range(2):
                    zo = 2 * d * half_w + cc * LANES
                    r = jax.nn.sigmoid(z[:, zo:zo + LANES] + br[d:d + 1, cs])
                    i = jax.nn.sigmoid(z[:, zo + half_w:zo + half_w + LANES] + bi[d:d + 1, cs])
                    w = c_sp[d:d + 1, cs] * r
                    a = jnp.exp2(w * (-LOG2E))
                    v = jnp.tanh(w) * (a * a + 1.0)
                    mult = jnp.where(v > 0.0, v * lax.rsqrt(v), 0.0)
                    a_scr[d][c, dst, :] = a
                    b_scr[d][c, dst, :] = mult * (i * xc[c])
        return 0

    lax.fori_loop(0, seq // GATE_ROWS, gate_chunk, 0, unroll=True)

    def scan_step(t, carry):
        hf, pf, hb, pb = carry
        idx_f = pl.ds(t, N_SEG, stride=sst)
        idx_b = pl.ds(seg - 1 - t, N_SEG, stride=sst)
        nhf, npf, nhb, npb = [], [], [], []
        for c in range(n_col):
            af = a_f[c, idx_f, :]
            h = af * hf[c] + b_f[c, idx_f, :]
            p = af * pf[c]
            b_f[c, idx_f, :] = h
            a_f[c, idx_f, :] = p
            nhf.append(h)
            npf.append(p)
            ab = a_b[c, idx_b, :]
            h = ab * hb[c] + b_b[c, idx_b, :]
            p = ab * pb[c]
            b_b[c, idx_b, :] = h
            a_b[c, idx_b, :] = p
            nhb.append(h)
            npb.append(p)
        return tuple(nhf), tuple(npf), tuple(nhb), tuple(npb)

    zero = tuple(jnp.zeros((N_SEG, LANES), F32) for _ in range(n_col))
    one = tuple(jnp.ones((N_SEG, LANES), F32) for _ in range(n_col))
    hf, pf, hb, pb = lax.fori_loop(0, seg, scan_step, (zero, one, zero, one), unroll=4)

    for c in range(n_col):
        cin = jnp.zeros((1, LANES), F32)
        for j in range(N_SEG):
            cf_ref[c, j:j + 1, :] = cin
            cin = hf[c][j:j + 1, :] + pf[c][j:j + 1, :] * cin
        cin = jnp.zeros((1, LANES), F32)
        for j in range(N_SEG - 1, -1, -1):
            cbk_ref[c, j:j + 1, :] = cin
            cin = hb[c][j:j + 1, :] + pb[c][j:j + 1, :] * cin

    gain = g_ref[...]

    def out_chunk(ci, _):
        r0 = pl.multiple_of(ci * OUT_ROWS, OUT_ROWS)
        rows = pl.ds(r0, OUT_ROWS)
        j = r0 // seg
        src = pl.ds(r0 + j * SEG_SKEW, OUT_ROWS)
        y = jnp.concatenate(
            [b_f[c, src, :] + a_f[c, src, :] * cf_ref[c, pl.ds(j, 1), :]
             + b_b[c, src, :] + a_b[c, src, :] * cbk_ref[c, pl.ds(j, 1), :]
             for c in range(n_col)], axis=-1)
        out = y * gl_ref[rows, :]
        o_ref[rows, :] = _rms(out, gain).astype(BF16)
        return 0

    lax.fori_loop(0, seq // OUT_ROWS, out_chunk, 0, unroll=True)


def _lru(xl, gl, conv_w, conv_b, wg, b_r, b_i, lam, gain, w_down, w_out, batch, seq):
    t = xl.shape[0]
    d_ff = w_down.shape[0]
    row = lambda b: (b, 0)
    const2 = lambda b: (0, 0)
    return pl.pallas_call(
        _lru_kernel,
        grid=(batch,),
        in_specs=[
            pl.BlockSpec((seq, LRU_WIDTH), row),
            pl.BlockSpec((seq, LRU_WIDTH), row),
            pl.BlockSpec((CONV_WIDTH, LRU_WIDTH), const2),
            pl.BlockSpec((1, LRU_WIDTH), const2),
            pl.BlockSpec((2, LRU_WIDTH // 2, 2 * LRU_WIDTH), lambda b: (0, 0, 0)),
            pl.BlockSpec((2, LRU_WIDTH), const2),
            pl.BlockSpec((2, LRU_WIDTH), const2),
            pl.BlockSpec((2, LRU_WIDTH), const2),
            pl.BlockSpec((1, LRU_WIDTH), const2),
            pl.BlockSpec((d_ff // batch, D_MODEL), row),
            pl.BlockSpec((D_MODEL // batch, D_MODEL), row),
        ],
        out_specs=[
            pl.BlockSpec((seq, LRU_WIDTH), row),
            pl.BlockSpec((d_ff // batch, D_MODEL), row),
            pl.BlockSpec((D_MODEL // batch, D_MODEL), row),
        ],
        out_shape=[
            jax.ShapeDtypeStruct((t, LRU_WIDTH), BF16),
            jax.ShapeDtypeStruct((d_ff, D_MODEL), BF16),
            jax.ShapeDtypeStruct((D_MODEL, D_MODEL), BF16),
        ],
        scratch_shapes=[
            pltpu.VMEM((LRU_WIDTH // LANES, seq + 2 * SUBLANES, LANES), F32),
            pltpu.VMEM((LRU_WIDTH // LANES, seq + N_SEG * SEG_SKEW, LANES), F32),
            pltpu.VMEM((LRU_WIDTH // LANES, seq + N_SEG * SEG_SKEW, LANES), F32),
            pltpu.VMEM((LRU_WIDTH // LANES, seq + N_SEG * SEG_SKEW, LANES), F32),
            pltpu.VMEM((LRU_WIDTH // LANES, seq + N_SEG * SEG_SKEW, LANES), F32),
            pltpu.VMEM((LRU_WIDTH // LANES, N_SEG, LANES), F32),
            pltpu.VMEM((LRU_WIDTH // LANES, N_SEG, LANES), F32),
        ],
        compiler_params=_params(1),
        name="rglru",
    )(xl, gl, conv_w, conv_b, wg, b_r, b_i, lam, gain, w_down, w_out)


def _ffn_kernel(x_ref, at_ref, lr_ref, wo_ref, g_ref, wg_ref, wu_ref, wd_ref, o_ref, ff_ref):
    tn = (((0,), (0,)), ((), ()))
    h1 = (x_ref[...]
          + lax.dot_general(at_ref[0], wo_ref[0:ATTN_WIDTH, :], tn, preferred_element_type=F32)
          + jnp.dot(lr_ref[...], wo_ref[ATTN_WIDTH:, :], preferred_element_type=F32))
    u = (h1 * g_ref[...]).astype(BF16)
    inv = lax.rsqrt(jnp.mean(h1 * h1, axis=-1, keepdims=True) + EPS)
    o_ref[...] = h1

    for c in range(wg_ref.shape[1] // FF_CHUNK):
        cols = slice(c * FF_CHUNK, (c + 1) * FF_CHUNK)
        gate = jnp.dot(u, wg_ref[:, cols], preferred_element_type=F32) * inv
        up = jnp.dot(u, wu_ref[:, cols], preferred_element_type=F32) * inv
        ff_ref[:, cols] = (jax.nn.silu(gate) * up).astype(BF16)

    o_ref[...] += jnp.dot(ff_ref[...], wd_ref[...], preferred_element_type=F32)


def _ffn(x2, attn_nt, lru_n, w_out, norm_g, wg, wu, wd):
    t = x2.shape[0]
    tm = TM_FFN
    d_ff = wg.shape[1]
    per_seq = attn_nt.shape[2] // tm
    row = lambda i: (i, 0)
    const2 = lambda i: (0, 0)
    once = pl.Buffered(1)
    return pl.pallas_call(
        _ffn_kernel,
        grid=(t // tm,),
        in_specs=[
            pl.BlockSpec((tm, D_MODEL), row),
            pl.BlockSpec((1, ATTN_WIDTH, tm), lambda i: (i // per_seq, 0, i % per_seq)),
            pl.BlockSpec((tm, LRU_WIDTH), row),
            pl.BlockSpec((D_MODEL, D_MODEL), const2, pipeline_mode=once),
            pl.BlockSpec((1, D_MODEL), const2),
            pl.BlockSpec((D_MODEL, d_ff), const2, pipeline_mode=once),
            pl.BlockSpec((D_MODEL, d_ff), const2, pipeline_mode=once),
            pl.BlockSpec((d_ff, D_MODEL), const2, pipeline_mode=once),
        ],
        out_specs=pl.BlockSpec((tm, D_MODEL), row),
        out_shape=jax.ShapeDtypeStruct((t, D_MODEL), F32),
        scratch_shapes=[pltpu.VMEM((tm, d_ff), BF16)],
        compiler_params=_params(1),
        name="outproj_ffn",
    )(x2, attn_nt, lru_n, w_out, norm_g, wg, wu, wd)


def _rope_tables(seq):
    f32 = np.float32
    rows = seq // GRID_W
    row = np.repeat(np.arange(rows, dtype=f32), GRID_W)
    col = np.tile(np.arange(GRID_W, dtype=f32), rows)
    axis_dim = HEAD_DIM // 2
    inv = np.power(f32(ROPE_THETA), -np.arange(0, axis_dim, 2, dtype=f32) / f32(axis_dim)).astype(f32)
    ang_r = row[:, None] * inv[None, :]
    ang_c = col[:, None] * inv[None, :]
    ang = np.concatenate([ang_r, ang_r, ang_c, ang_c], axis=-1)
    ang = np.concatenate([ang, ang], axis=-1)
    first = (np.arange(LANES) % axis_dim) < (axis_dim // 2)
    cos = np.cos(ang).astype(f32)
    sin = np.sin(ang).astype(f32)
    sin_a = np.where(first[None, :], -sin, f32(0.0)).astype(f32)
    sin_b = np.where(first[None, :], f32(0.0), sin).astype(f32)
    return jnp.asarray(cos), jnp.asarray(sin_a), jnp.asarray(sin_b)


def _block_diag_halves(w):
    per = (LRU_WIDTH // 2) // LRU_BLOCK
    blocks = w.reshape(2, per, LRU_BLOCK, LRU_BLOCK)
    eye = jnp.eye(per, dtype=w.dtype)
    dense = blocks[:, :, :, None, :] * eye[None, :, None, :, None]
    return dense.reshape(2, LRU_WIDTH // 2, LRU_WIDTH // 2)


def kernel(x, norm_mix, w_in, q_norm, k_norm, conv_w, conv_b, w_rgate, b_rgate, w_igate, b_igate, lru_lambda, out_norm_attn, out_norm_lru, w_out, norm_ffn, w_gate, w_up, w_down):
    batch, seq, _ = x.shape
    depth = norm_mix.shape[0]
    cos2, sin_a, sin_b = _rope_tables(seq)
    head_id = jnp.arange(LANES) // HEAD_DIM
    bd = (head_id[:, None] == head_id[None, :]).astype(BF16)
    bd = jnp.concatenate([bd, bd], axis=0)

    h = x.reshape(batch * seq, D_MODEL)
    for l in range(depth):
        qg = jnp.tile(q_norm[l] * (HEAD_DIM ** -0.5 * LOG2E), 2)[None, :]
        kg = jnp.tile(k_norm[l], 2)[None, :]
        q, k, vt, xl, gl, w_gate16, w_up16 = _inproj(
            h, norm_mix[l][None, :], w_in[l], cos2, sin_a, sin_b, qg, kg, bd,
            w_gate[l], w_up[l], batch, seq)
        bound = (BOUND_MARGIN * HEAD_DIM * (HEAD_DIM ** -0.5 * LOG2E)
                 * jnp.max(jnp.abs(q_norm[l])) * jnp.max(jnp.abs(k_norm[l]))).reshape(1, 1)
        attn_gain = jnp.broadcast_to(out_norm_attn[l][:, None], (ATTN_WIDTH, LANES))
        attn_nt = _attention(bound, q, k, vt, attn_gain, batch, seq)

        wg = jnp.concatenate([_block_diag_halves(w_rgate[l, 0]), _block_diag_halves(w_igate[l, 0]),
                              _block_diag_halves(w_rgate[l, 1]), _block_diag_halves(w_igate[l, 1])],
                             axis=2)
        wg = wg.astype(BF16)
        lru_n, w_down16, w_out16 = _lru(
            xl, gl, conv_w[l], conv_b[l][None, :], wg, b_rgate[l], b_igate[l],
            lru_lambda[l], out_norm_lru[l][None, :], w_down[l], w_out[l], batch, seq)

        h = _ffn(h, attn_nt, lru_n, w_out16, norm_ffn[l][None, :], w_gate16, w_up16, w_down16)
    return h.reshape(batch, seq, D_MODEL)
```

```python
import jax
import jax.numpy as jnp
import numpy as np
from jax import lax
from jax.experimental import pallas as pl
from jax.experimental.pallas import tpu as pltpu

F32 = jnp.float32
BF16 = jnp.bfloat16

D_MODEL = 1024
GRID_W = 64
HEAD_DIM = 64
N_Q_HEADS = 8
N_KV_HEADS = 2
ATTN_WIDTH = N_Q_HEADS * HEAD_DIM
KV_WIDTH = N_KV_HEADS * HEAD_DIM
LRU_WIDTH = D_MODEL - ATTN_WIDTH
LRU_BLOCKS = 8
LRU_BLOCK = LRU_WIDTH // LRU_BLOCKS
CONV_WIDTH = 4
LRU_C = 8.0
D_IN = ATTN_WIDTH + 2 * KV_WIDTH + 2 * LRU_WIDTH
ROPE_THETA = 10000.0
EPS = 1e-6

LANES = 128
SUBLANES = 8
MXU_DIM = 256
VMEM_LIMIT = 56 * 1024 * 1024

TM_PROJ = 1024
TM_FFN = 1024
TQ = 512
SHIFT_BOUND_MAX = 40.0
BOUND_MARGIN = 1.02
FF_CHUNK = MXU_DIM
GATE_ROWS = 256
OUT_ROWS = 256
N_SEG = SUBLANES
SEG_SKEW = 1
LOG2E = 1.4426950408889634
V_ROWS = HEAD_DIM + 16


def _params(n_grid_axes):
    return pltpu.CompilerParams(dimension_semantics=("arbitrary",) * n_grid_axes,
                                vmem_limit_bytes=VMEM_LIMIT)


def _rms(x, gain):
    return x * lax.rsqrt(jnp.mean(x * x, axis=-1, keepdims=True) + EPS) * gain


def _inproj_kernel(x_ref, g_ref, w32_ref, cos_ref, sa_ref, sb_ref, qg_ref, kg_ref, bd_ref,
                   wgate32_ref, wup32_ref,
                   q_ref, k_ref, vt_ref, xl_ref, gl_ref, wgate_ref, wup_ref, w_ref):
    @pl.when(pl.program_id(0) == 0)
    def _():
        w_ref[...] = w32_ref[...].astype(BF16)

    wgate_ref[...] = wgate32_ref[...].astype(BF16)
    wup_ref[...] = wup32_ref[...].astype(BF16)

    u = _rms(x_ref[...], g_ref[...]).astype(BF16)
    cos = cos_ref[...]
    sin_a = sa_ref[...]
    sin_b = sb_ref[...]
    bd = bd_ref[...]
    lane = lax.broadcasted_iota(jnp.int32, (1, LANES), 1)
    low = lane < HEAD_DIM

    def norm_rope(z, gain):
        sq = z * z
        hi = sq.astype(BF16)
        lo = (sq - hi.astype(F32)).astype(BF16)
        ss = jnp.dot(jnp.concatenate([hi, lo], axis=-1), bd, preferred_element_type=F32)
        y = z * lax.rsqrt(ss * (1.0 / HEAD_DIM) + EPS) * gain
        return (y * cos + pltpu.roll(y, LANES - 16, 1) * sin_a
                + pltpu.roll(y, 16, 1) * sin_b)

    def proj(start, width):
        return jnp.dot(u, w_ref[:, start:start + width], preferred_element_type=F32)

    qg = qg_ref[...]
    zq = proj(0, ATTN_WIDTH)
    for c in range(ATTN_WIDTH // LANES):
        q_ref[:, c * LANES:(c + 1) * LANES] = norm_rope(
            zq[:, c * LANES:(c + 1) * LANES], qg).astype(BF16)

    o = ATTN_WIDTH
    zkv = proj(o, 2 * KV_WIDTH)
    k2 = norm_rope(zkv[:, 0:KV_WIDTH], kg_ref[...])
    v2 = zkv[:, KV_WIDTH:2 * KV_WIDTH]
    o += 2 * KV_WIDTH

    a0 = jnp.where(low, k2, 0.0)
    b1 = jnp.where(low, 0.0, k2)
    k_ref[...] = jnp.concatenate(
        [a0, pltpu.roll(a0, HEAD_DIM, 1), pltpu.roll(b1, HEAD_DIM, 1), b1], axis=-1).astype(BF16)

    vt = v2.T.astype(BF16)
    ones = jnp.ones((V_ROWS - HEAD_DIM, vt.shape[1]), BF16)
    for g in range(N_KV_HEADS):
        vt_ref[0, g, 0:HEAD_DIM, :] = vt[g * HEAD_DIM:(g + 1) * HEAD_DIM, :]
        vt_ref[0, g, HEAD_DIM:V_ROWS, :] = ones
    gl_ref[...] = jax.nn.gelu(proj(o + LRU_WIDTH, LRU_WIDTH))
    xl_ref[...] = proj(o, LRU_WIDTH)


def _inproj(x2, norm_g, w_in, cos2, sin_a, sin_b, qg, kg, bd, w_gate, w_up, batch, seq):
    t = x2.shape[0]
    tm = TM_PROJ
    steps = t // tm
    per_seq = seq // tm
    d_ff = w_gate.shape[1]
    slab = D_MODEL // steps
    row = lambda i: (i, 0)
    const = lambda i: (0, 0)
    pos = lambda i: (i % per_seq, 0)
    return pl.pallas_call(
        _inproj_kernel,
        grid=(steps,),
        in_specs=[
            pl.BlockSpec((tm, D_MODEL), row),
            pl.BlockSpec((1, D_MODEL), const),
            pl.BlockSpec((D_MODEL, D_IN), const, pipeline_mode=pl.Buffered(1)),
            pl.BlockSpec((tm, LANES), pos),
            pl.BlockSpec((tm, LANES), pos),
            pl.BlockSpec((tm, LANES), pos),
            pl.BlockSpec((1, LANES), const),
            pl.BlockSpec((1, LANES), const),
            pl.BlockSpec((2 * LANES, LANES), const),
            pl.BlockSpec((slab, d_ff), row),
            pl.BlockSpec((slab, d_ff), row),
        ],
        out_specs=[
            pl.BlockSpec((tm, ATTN_WIDTH), row),
            pl.BlockSpec((tm, 4 * LANES), row),
            pl.BlockSpec((1, N_KV_HEADS, V_ROWS, tm), lambda i: (i // per_seq, 0, 0, i % per_seq)),
            pl.BlockSpec((tm, LRU_WIDTH), row),
            pl.BlockSpec((tm, LRU_WIDTH), row),
            pl.BlockSpec((slab, d_ff), row),
            pl.BlockSpec((slab, d_ff), row),
        ],
        out_shape=[
            jax.ShapeDtypeStruct((t, ATTN_WIDTH), BF16),
            jax.ShapeDtypeStruct((t, 4 * LANES), BF16),
            jax.ShapeDtypeStruct((batch, N_KV_HEADS, V_ROWS, seq), BF16),
            jax.ShapeDtypeStruct((t, LRU_WIDTH), F32),
            jax.ShapeDtypeStruct((t, LRU_WIDTH), F32),
            jax.ShapeDtypeStruct((D_MODEL, d_ff), BF16),
            jax.ShapeDtypeStruct((D_MODEL, d_ff), BF16),
        ],
        scratch_shapes=[pltpu.VMEM((D_MODEL, D_IN), BF16)],
        compiler_params=_params(1),
        name="inproj",
    )(x2, norm_g, w_in, cos2, sin_a, sin_b, qg, kg, bd, w_gate, w_up)


def _attn_kernel(bound_ref, q_ref, k_ref, vt_ref, g_ref, o_ref, ot_ref, s0_ref, s1_ref):
    nt = (((1,), (1,)), ((), ()))
    group = N_Q_HEADS // N_KV_HEADS
    s_slots = (s0_ref, s1_ref)

    def scores(h):
        col = (2 * (h // group) + h % 2) * LANES
        return lax.dot_general(k_ref[:, col:col + LANES], q_ref[:, (h // 2) * LANES:(h // 2 + 1) * LANES],
                               nt, preferred_element_type=F32)

    def weighted_values(h, p):
        ot = jnp.dot(vt_ref[0, h // group], p, preferred_element_type=F32)
        inv = 1.0 / ot[HEAD_DIM:HEAD_DIM + 1, :]
        ot_ref[h * HEAD_DIM:(h + 1) * HEAD_DIM, :] = ot[0:HEAD_DIM, :] * inv

    bound = bound_ref[0, 0]

    @pl.when(bound <= SHIFT_BOUND_MAX)
    def _():
        for h in range(N_Q_HEADS):
            weighted_values(h, jnp.exp2(scores(h)).astype(BF16))

    @pl.when(jnp.logical_not(bound <= SHIFT_BOUND_MAX))
    def _():
        def stage_scores(h):
            s = scores(h)
            s_slots[h % 2][...] = s
            return jnp.max(s, axis=0, keepdims=True)

        m = stage_scores(0)
        for h in range(N_Q_HEADS):
            m_next = stage_scores(h + 1) if h + 1 < N_Q_HEADS else None
            weighted_values(h, jnp.exp2(s_slots[h % 2][...] - m).astype(BF16))
            m = m_next

    ot = ot_ref[...]
    scale = lax.rsqrt(jnp.mean(ot * ot, axis=0, keepdims=True) + EPS)
    for c in range(ot.shape[1] // LANES):
        cols = slice(c * LANES, (c + 1) * LANES)
        o_ref[0, :, cols] = (ot[:, cols] * scale[:, cols] * g_ref[...]).astype(BF16)


def _attention(bound, q, k, vt, gain, batch, seq):
    t = q.shape[0]
    nq = seq // TQ
    return pl.pallas_call(
        _attn_kernel,
        grid=(batch, nq),
        in_specs=[
            pl.BlockSpec(memory_space=pltpu.SMEM),
            pl.BlockSpec((TQ, ATTN_WIDTH), lambda b, i: (b * nq + i, 0)),
            pl.BlockSpec((seq, 4 * LANES), lambda b, i: (b, 0)),
            pl.BlockSpec((1, N_KV_HEADS, V_ROWS, seq), lambda b, i: (b, 0, 0, 0)),
            pl.BlockSpec((ATTN_WIDTH, LANES), lambda b, i: (0, 0)),
        ],
        out_specs=pl.BlockSpec((1, ATTN_WIDTH, TQ), lambda b, i: (b, 0, i)),
        out_shape=jax.ShapeDtypeStruct((batch, ATTN_WIDTH, seq), BF16),
        scratch_shapes=[pltpu.VMEM((ATTN_WIDTH, TQ), F32), pltpu.VMEM((seq, TQ), F32),
                        pltpu.VMEM((seq, TQ), F32)],
        compiler_params=_params(2),
        name="attention",
    )(bound, q, k, vt, gain)


def _lru_kernel(xl_ref, gl_ref, cw_ref, cb_ref, wg_ref, br_ref, bi_ref, lam_ref, g_ref,
                wdown32_ref, wout32_ref, o_ref, wdown_ref, wout_ref,
                xcol, a_f, b_f, a_b, b_b, cf_ref, cbk_ref):
    wdown_ref[...] = wdown32_ref[...].astype(BF16)
    wout_ref[...] = wout32_ref[...].astype(BF16)

    seq = xl_ref.shape[0]
    seg = seq // N_SEG
    sst = seg + SEG_SKEW
    n_col = LRU_WIDTH // LANES
    half_cols = n_col // 2
    half_w = LRU_WIDTH // 2
    pad = SUBLANES

    zeros_pad = jnp.zeros((pad, LANES), F32)
    for c in range(n_col):
        xcol[c, 0:pad, :] = zeros_pad
        xcol[c, pad + seq:pad + seq + pad, :] = zeros_pad
        xcol[c, pad:pad + seq, :] = xl_ref[:, c * LANES:(c + 1) * LANES]

    lam = lam_ref[...]
    c_sp = LRU_C * (jnp.maximum(-lam, 0.0) + jnp.log1p(jnp.exp(-jnp.abs(lam))))
    cw = cw_ref[...]
    cbias = cb_ref[...]
    br = br_ref[...]
    bi = bi_ref[...]
    a_scr = (a_f, a_b)
    b_scr = (b_f, b_b)

    def gate_chunk(ci, _):
        r0 = pl.multiple_of(ci * GATE_ROWS, GATE_ROWS)
        dst = pl.ds(r0 + (r0 // seg) * SEG_SKEW, GATE_ROWS)
        xc = []
        for c in range(n_col):
            cs = slice(c * LANES, (c + 1) * LANES)
            acc = cbias[:, cs]
            for k in range(CONV_WIDTH):
                acc = acc + xcol[c, pl.ds(r0 + pad - 1 + k, GATE_ROWS), :] * cw[k:k + 1, cs]
            xc.append(acc)
        for h in range(2):
            lhs = jnp.concatenate(xc[h * half_cols:(h + 1) * half_cols], axis=-1).astype(BF16)
            z = jnp.dot(lhs, wg_ref[h], preferred_element_type=F32)
            for cc in range(half_cols):
                c = h * half_cols + cc
                cs = slice(c * LANES, (c + 1) * LANES)
                for d in range(2):
                    zo = 2 * d * half_w + cc * LANES
                    r = jax.nn.sigmoid(z[:, zo:zo + LANES] + br[d:d + 1, cs])
                    i = jax.nn.sigmoid(z[:, zo + half_w:zo + half_w + LANES] + bi[d:d + 1, cs])
                    w = c_sp[d:d + 1, cs] * r
                    a = jnp.exp2(w * (-LOG2E))
                    v = jnp.tanh(w) * (a * a + 1.0)
                    mult = jnp.where(v > 0.0, v * lax.rsqrt(v), 0.0)
                    a_scr[d][c, dst, :] = a
                    b_scr[d][c, dst, :] = mult * (i * xc[c])
        return 0

    lax.fori_loop(0, seq // GATE_ROWS, gate_chunk, 0, unroll=True)

    def scan_step(t, carry):
        hf, pf, hb, pb = carry
        idx_f = pl.ds(t, N_SEG, stride=sst)
        idx_b = pl.ds(seg - 1 - t, N_SEG, stride=sst)
        nhf, npf, nhb, npb = [], [], [], []
        for c in range(n_col):
            af = a_f[c, idx_f, :]
            h = af * hf[c] + b_f[c, idx_f, :]
            p = af * pf[c]
            b_f[c, idx_f, :] = h
            a_f[c, idx_f, :] = p
            nhf.append(h)
            npf.append(p)
            ab = a_b[c, idx_b, :]
            h = ab * hb[c] + b_b[c, idx_b, :]
            p = ab * pb[c]
            b_b[c, idx_b, :] = h
            a_b[c, idx_b, :] = p
            nhb.append(h)
            npb.append(p)
        return tuple(nhf), tuple(npf), tuple(nhb), tuple(npb)

    zero = tuple(jnp.zeros((N_SEG, LANES), F32) for _ in range(n_col))
    one = tuple(jnp.ones((N_SEG, LANES), F32) for _ in range(n_col))
    hf, pf, hb, pb = lax.fori_loop(0, seg, scan_step, (zero, one, zero, one), unroll=4)

    for c in range(n_col):
        cin = jnp.zeros((1, LANES), F32)
        for j in range(N_SEG):
            cf_ref[c, j:j + 1, :] = cin
            cin = hf[c][j:j + 1, :] + pf[c][j:j + 1, :] * cin
        cin = jnp.zeros((1, LANES), F32)
        for j in range(N_SEG - 1, -1, -1):
            cbk_ref[c, j:j + 1, :] = cin
            cin = hb[c][j:j + 1, :] + pb[c][j:j + 1, :] * cin

    gain = g_ref[...]

    def out_chunk(ci, _):
        r0 = pl.multiple_of(ci * OUT_ROWS, OUT_ROWS)
        rows = pl.ds(r0, OUT_ROWS)
        j = r0 // seg
        src = pl.ds(r0 + j * SEG_SKEW, OUT_ROWS)
        y = jnp.concatenate(
            [b_f[c, src, :] + a_f[c, src, :] * cf_ref[c, pl.ds(j, 1), :]
             + b_b[c, src, :] + a_b[c, src, :] * cbk_ref[c, pl.ds(j, 1), :]
             for c in range(n_col)], axis=-1)
        out = y * gl_ref[rows, :]
        o_ref[rows, :] = _rms(out, gain).astype(BF16)
        return 0

    lax.fori_loop(0, seq // OUT_ROWS, out_chunk, 0, unroll=True)


def _lru(xl, gl, conv_w, conv_b, wg, b_r, b_i, lam, gain, w_down, w_out, batch, seq):
    t = xl.shape[0]
    d_ff = w_down.shape[0]
    row = lambda b: (b, 0)
    const2 = lambda b: (0, 0)
    return pl.pallas_call(
        _lru_kernel,
        grid=(batch,),
        in_specs=[
            pl.BlockSpec((seq, LRU_WIDTH), row),
            pl.BlockSpec((seq, LRU_WIDTH), row),
            pl.BlockSpec((CONV_WIDTH, LRU_WIDTH), const2),
            pl.BlockSpec((1, LRU_WIDTH), const2),
            pl.BlockSpec((2, LRU_WIDTH // 2, 2 * LRU_WIDTH), lambda b: (0, 0, 0)),
            pl.BlockSpec((2, LRU_WIDTH), const2),
            pl.BlockSpec((2, LRU_WIDTH), const2),
            pl.BlockSpec((2, LRU_WIDTH), const2),
            pl.BlockSpec((1, LRU_WIDTH), const2),
            pl.BlockSpec((d_ff // batch, D_MODEL), row),
            pl.BlockSpec((D_MODEL // batch, D_MODEL), row),
        ],
        out_specs=[
            pl.BlockSpec((seq, LRU_WIDTH), row),
            pl.BlockSpec((d_ff // batch, D_MODEL), row),
            pl.BlockSpec((D_MODEL // batch, D_MODEL), row),
        ],
        out_shape=[
            jax.ShapeDtypeStruct((t, LRU_WIDTH), BF16),
            jax.ShapeDtypeStruct((d_ff, D_MODEL), BF16),
            jax.ShapeDtypeStruct((D_MODEL, D_MODEL), BF16),
        ],
        scratch_shapes=[
            pltpu.VMEM((LRU_WIDTH // LANES, seq + 2 * SUBLANES, LANES), F32),
            pltpu.VMEM((LRU_WIDTH // LANES, seq + N_SEG * SEG_SKEW, LANES), F32),
            pltpu.VMEM((LRU_WIDTH // LANES, seq + N_SEG * SEG_SKEW, LANES), F32),
            pltpu.VMEM((LRU_WIDTH // LANES, seq + N_SEG * SEG_SKEW, LANES), F32),
            pltpu.VMEM((LRU_WIDTH // LANES, seq + N_SEG * SEG_SKEW, LANES), F32),
            pltpu.VMEM((LRU_WIDTH // LANES, N_SEG, LANES), F32),
            pltpu.VMEM((LRU_WIDTH // LANES, N_SEG, LANES), F32),
        ],
        compiler_params=_params(1),
        name="rglru",
    )(xl, gl, conv_w, conv_b, wg, b_r, b_i, lam, gain, w_down, w_out)


def _ffn_kernel(x_ref, at_ref, lr_ref, wo_ref, g_ref, wg_ref, wu_ref, wd_ref, o_ref, ff_ref):
    tn = (((0,), (0,)), ((), ()))
    h1 = (x_ref[...]
          + lax.dot_general(at_ref[0], wo_ref[0:ATTN_WIDTH, :], tn, preferred_element_type=F32)
          + jnp.dot(lr_ref[...], wo_ref[ATTN_WIDTH:, :], preferred_element_type=F32))
    u = (h1 * g_ref[...]).astype(BF16)
    inv = lax.rsqrt(jnp.mean(h1 * h1, axis=-1, keepdims=True) + EPS)

    for c in range(wg_ref.shape[1] // FF_CHUNK):
        cols = slice(c * FF_CHUNK, (c + 1) * FF_CHUNK)
        gate = jnp.dot(u, wg_ref[:, cols], preferred_element_type=F32) * inv
        up = jnp.dot(u, wu_ref[:, cols], preferred_element_type=F32) * inv
        ff_ref[:, cols] = (jax.nn.silu(gate) * up).astype(BF16)

    o_ref[...] = h1 + jnp.dot(ff_ref[...], wd_ref[...], preferred_element_type=F32)


def _ffn(x2, attn_nt, lru_n, w_out, norm_g, wg, wu, wd):
    t = x2.shape[0]
    tm = TM_FFN
    d_ff = wg.shape[1]
    per_seq = attn_nt.shape[2] // tm
    row = lambda i: (i, 0)
    const2 = lambda i: (0, 0)
    once = pl.Buffered(1)
    return pl.pallas_call(
        _ffn_kernel,
        grid=(t // tm,),
        in_specs=[
            pl.BlockSpec((tm, D_MODEL), row),
            pl.BlockSpec((1, ATTN_WIDTH, tm), lambda i: (i // per_seq, 0, i % per_seq)),
            pl.BlockSpec((tm, LRU_WIDTH), row),
            pl.BlockSpec((D_MODEL, D_MODEL), const2, pipeline_mode=once),
            pl.BlockSpec((1, D_MODEL), const2),
            pl.BlockSpec((D_MODEL, d_ff), const2, pipeline_mode=once),
            pl.BlockSpec((D_MODEL, d_ff), const2, pipeline_mode=once),
            pl.BlockSpec((d_ff, D_MODEL), const2, pipeline_mode=once),
        ],
        out_specs=pl.BlockSpec((tm, D_MODEL), row),
        out_shape=jax.ShapeDtypeStruct((t, D_MODEL), F32),
        scratch_shapes=[pltpu.VMEM((tm, d_ff), BF16)],
        compiler_params=_params(1),
        name="outproj_ffn",
    )(x2, attn_nt, lru_n, w_out, norm_g, wg, wu, wd)


def _rope_tables(seq):
    f32 = np.float32
    rows = seq // GRID_W
    row = np.repeat(np.arange(rows, dtype=f32), GRID_W)
    col = np.tile(np.arange(GRID_W, dtype=f32), rows)
    axis_dim = HEAD_DIM // 2
    inv = np.power(f32(ROPE_THETA), -np.arange(0, axis_dim, 2, dtype=f32) / f32(axis_dim)).astype(f32)
    ang_r = row[:, None] * inv[None, :]
    ang_c = col[:, None] * inv[None, :]
    ang = np.concatenate([ang_r, ang_r, ang_c, ang_c], axis=-1)
    ang = np.concatenate([ang, ang], axis=-1)
    first = (np.arange(LANES) % axis_dim) < (axis_dim // 2)
    cos = np.cos(ang).astype(f32)
    sin = np.sin(ang).astype(f32)
    sin_a = np.where(first[None, :], -sin, f32(0.0)).astype(f32)
    sin_b = np.where(first[None, :], f32(0.0), sin).astype(f32)
    return jnp.asarray(cos), jnp.asarray(sin_a), jnp.asarray(sin_b)


def _block_diag_halves(w):
    per = (LRU_WIDTH // 2) // LRU_BLOCK
    blocks = w.reshape(2, per, LRU_BLOCK, LRU_BLOCK)
    eye = jnp.eye(per, dtype=w.dtype)
    dense = blocks[:, :, :, None, :] * eye[None, :, None, :, None]
    return dense.reshape(2, LRU_WIDTH // 2, LRU_WIDTH // 2)


def kernel(x, norm_mix, w_in, q_norm, k_norm, conv_w, conv_b, w_rgate, b_rgate, w_igate, b_igate, lru_lambda, out_norm_attn, out_norm_lru, w_out, norm_ffn, w_gate, w_up, w_down):
    batch, seq, _ = x.shape
    depth = norm_mix.shape[0]
    cos2, sin_a, sin_b = _rope_tables(seq)
    head_id = jnp.arange(LANES) // HEAD_DIM
    bd = (head_id[:, None] == head_id[None, :]).astype(BF16)
    bd = jnp.concatenate([bd, bd], axis=0)

    h = x.reshape(batch * seq, D_MODEL)
    for l in range(depth):
        qg = jnp.tile(q_norm[l] * (HEAD_DIM ** -0.5 * LOG2E), 2)[None, :]
        kg = jnp.tile(k_norm[l], 2)[None, :]
        q, k, vt, xl, gl, w_gate16, w_up16 = _inproj(
            h, norm_mix[l][None, :], w_in[l], cos2, sin_a, sin_b, qg, kg, bd,
            w_gate[l], w_up[l], batch, seq)
        bound = (BOUND_MARGIN * HEAD_DIM * (HEAD_DIM ** -0.5 * LOG2E)
                 * jnp.max(jnp.abs(q_norm[l])) * jnp.max(jnp.abs(k_norm[l]))).reshape(1, 1)
        attn_gain = jnp.broadcast_to(out_norm_attn[l][:, None], (ATTN_WIDTH, LANES))
        attn_nt = _attention(bound, q, k, vt, attn_gain, batch, seq)

        wg = jnp.concatenate([_block_diag_halves(w_rgate[l, 0]), _block_diag_halves(w_igate[l, 0]),
                              _block_diag_halves(w_rgate[l, 1]), _block_diag_halves(w_igate[l, 1])],
                             axis=2)
        wg = wg.astype(BF16)
        lru_n, w_down16, w_out16 = _lru(
            xl, gl, conv_w[l], conv_b[l][None, :], wg, b_rgate[l], b_igate[l],
            lru_lambda[l], out_norm_lru[l][None, :], w_down[l], w_out[l], batch, seq)

        h = _ffn(h, attn_nt, lru_n, w_out16, norm_ffn[l][None, :], w_gate16, w_up16, w_down16)
    return h.reshape(batch, seq, D_MODEL)
```
